```python
import math
import jax, jax.numpy as jnp
from jax import lax
import numpy as np

D_MODEL = 2048
BATCH = 8
SEQ = 2048
DEPTH = 4
DEC_BATCH = 16
DEC_SEQ = 64
PAST_LEN = 4096

CHUNK = 64
MIX_WIDTH = D_MODEL
CONV_CH = MIX_WIDTH // 2
CONV_GROUPS = 8
CONV_DW_WIDTH = 31
HEAD_DIM = 128
GDN_WIDTH = MIX_WIDTH - CONV_CH
GDN_HEADS = GDN_WIDTH // HEAD_DIM
SHORT_CONV = 4
FFN_DIM = 11 * D_MODEL // 4
FFN_CONV = 3
EPS = 1e-6
IN_COLS = 2 * CONV_CH + 4 * GDN_WIDTH + 2 * GDN_HEADS

kernel_name = "hymba_conformer_gdn_convffn_stream"


def rmsnorm(x, g):
    xf = x.astype(jnp.float32)
    y = xf * lax.rsqrt(jnp.mean(xf * xf, axis=-1, keepdims=True) + EPS)
    return (y * g.astype(jnp.float32)).astype(x.dtype)


def l2norm(x):
    return x * lax.rsqrt(jnp.sum(x * x, axis=-1, keepdims=True) + EPS)


def causal_dwconv(x, buf, w, b):
    K = w.shape[0]
    xp = jnp.concatenate([buf.astype(x.dtype), x], axis=1)
    y = lax.conv_general_dilated(xp, w[:, None, :].astype(x.dtype), window_strides=(1,),
                                 padding='VALID', dimension_numbers=('NWC', 'WIO', 'NWC'),
                                 feature_group_count=x.shape[-1])
    if b is not None:
        y = y + b.astype(y.dtype)
    return y, xp[:, -(K - 1):]


def conformer_conv(h_val, h_gate, buf, w_dw, b_dw, gn_g, gn_b):
    u = h_val * jax.nn.sigmoid(h_gate)
    u, new_buf = causal_dwconv(u, buf, w_dw, b_dw)
    B, T, C = u.shape
    uf = u.astype(jnp.float32).reshape(B, T, CONV_GROUPS, C // CONV_GROUPS)
    mu = jnp.mean(uf, axis=-1, keepdims=True)
    var = jnp.mean(jnp.square(uf - mu), axis=-1, keepdims=True)
    un = ((uf - mu) * lax.rsqrt(var + EPS)).reshape(B, T, C) * gn_g.astype(jnp.float32) + gn_b.astype(jnp.float32)
    return jax.nn.silu(un).astype(h_val.dtype), new_buf


def gated_delta_chunked(q, k, v, g, beta, s0):
    L = q.shape[-2]
    G = jnp.cumsum(g, axis=-1)
    incl = jnp.tril(jnp.ones((L, L), dtype=bool))
    strict = jnp.tril(jnp.ones((L, L), dtype=bool), -1)
    decay = jnp.exp(jnp.where(incl, G[..., :, None] - G[..., None, :], -jnp.inf))
    kb = k * beta[..., None]
    m = jnp.where(strict, jnp.einsum('bhnid,bhnjd->bhnij', kb, k) * decay, 0.0)
    eye = jnp.eye(L, dtype=jnp.float32)
    tmat = lax.linalg.triangular_solve(m + eye, jnp.broadcast_to(eye, m.shape), left_side=True,
                                       lower=True, unit_diagonal=True)
    u = jnp.einsum('bhnij,bhnjd->bhnid', tmat, v * beta[..., None])
    w = jnp.einsum('bhnij,bhnjd->bhnid', tmat, kb * jnp.exp(G)[..., None])
    qk = jnp.einsum('bhnid,bhnjd->bhnij', q, k) * decay
    q_dec = q * jnp.exp(G)[..., None]
    k_dec = k * jnp.exp(G[..., -1:] - G)[..., None]
    g_last = jnp.exp(G[..., -1])

    def step(s, xs):
        u_c, w_c, qk_c, qd_c, kd_c, gl_c = xs
        v_new = u_c - jnp.einsum('bhld,bhde->bhle', w_c, s)
        o = jnp.einsum('bhld,bhde->bhle', qd_c, s) + jnp.einsum('bhij,bhje->bhie', qk_c, v_new)
        s = s * gl_c[..., None, None] + jnp.einsum('bhld,bhle->bhde', kd_c, v_new)
        return s, o

    xs = tuple(jnp.moveaxis(t, 2, 0) for t in (u, w, qk, q_dec, k_dec, g_last))
    s_final, o = lax.scan(step, s0, xs)
    return jnp.moveaxis(o, 0, 2), s_final


def gdn_mixer(qkv, z, b_raw, a_raw, buf, s0, w_sc, a_log, dt_bias, onorm_g, chunk):
    qkv, new_buf = causal_dwconv(qkv, buf, w_sc, None)
    qkv = jax.nn.silu(qkv.astype(jnp.float32))
    B, T, _ = qkv.shape
    n = T // chunk
    q, k, v = jnp.split(qkv, 3, axis=-1)

    def heads(t):
        return t.reshape(B, n, chunk, GDN_HEADS, HEAD_DIM).transpose(0, 3, 1, 2, 4)

    def hchunks(t):
        return t.reshape(B, n, chunk, GDN_HEADS).transpose(0, 3, 1, 2)

    q = l2norm(heads(q)) * (HEAD_DIM ** -0.5)
    k = l2norm(heads(k))
    v = heads(v)
    beta = jax.nn.sigmoid(b_raw.astype(jnp.float32))
    g = -jnp.exp(a_log.astype(jnp.float32)) * jax.nn.softplus(a_raw.astype(jnp.float32) + dt_bias.astype(jnp.float32))
    o, s = gated_delta_chunked(q, k, v, hchunks(g), hchunks(beta), s0.astype(jnp.float32))
    o = o.transpose(0, 2, 3, 1, 4).reshape(B, T, GDN_HEADS, HEAD_DIM)
    o = o * lax.rsqrt(jnp.mean(o * o, axis=-1, keepdims=True) + EPS) * onorm_g.astype(jnp.float32)
    o = o.reshape(B, T, GDN_WIDTH) * jax.nn.silu(z.astype(jnp.float32))
    return o.astype(z.dtype), new_buf, s


def trunk_layer(x, conv_buf, qkv_buf, s0, ffn_buf, chunk,
                g_pre_mix, w_in, w_dw, b_dw, gn_g, gn_b, w_sc, a_log, dt_bias, onorm_g,
                w_out, g_post_mix, g_pre_ffn, w_up, w_ffn_dw, b_ffn_dw, w_down, g_post_ffn):
    h = rmsnorm(x, g_pre_mix)
    p = h @ w_in
    c0 = CONV_CH
    c1 = 2 * CONV_CH
    c2 = c1 + 3 * GDN_WIDTH
    c3 = c2 + GDN_WIDTH
    c4 = c3 + GDN_HEADS
    a_out, conv_buf = conformer_conv(p[..., :c0], p[..., c0:c1], conv_buf, w_dw, b_dw, gn_g, gn_b)
    b_out, qkv_buf, s = gdn_mixer(p[..., c1:c2], p[..., c2:c3], p[..., c3:c4], p[..., c4:],
                                  qkv_buf, s0, w_sc, a_log, dt_bias, onorm_g, chunk)
    x = x + rmsnorm(jnp.concatenate([a_out, b_out], axis=-1) @ w_out, g_post_mix)
    h = rmsnorm(x, g_pre_ffn)
    gu, ffn_buf = causal_dwconv(h @ w_up, ffn_buf, w_ffn_dw, b_ffn_dw)
    gate, up = jnp.split(gu, 2, axis=-1)
    x = x + rmsnorm((jax.nn.silu(gate) * up) @ w_down, g_post_ffn)
    return x, conv_buf, qkv_buf, s, ffn_buf


def setup_inputs(seed: int = 0) -> dict:
    key = jax.random.key(seed)
    ks = jax.random.split(key, 32)
    f32 = jnp.float32

    def nrm(k, shape, s):
        return jax.random.normal(k, shape, f32) * s

    L = DEPTH
    a_val = jax.random.uniform(ks[10], (L, GDN_HEADS), f32, 1.0, 16.0)
    dtv = jnp.exp(jax.random.uniform(ks[11], (L, GDN_HEADS), f32, math.log(1e-3), math.log(0.1)))
    dt_bias = dtv + jnp.log(-jnp.expm1(-dtv))
    return {
        "x_prompt": nrm(ks[0], (BATCH, SEQ, D_MODEL), 1.0),
        "x_sample": nrm(ks[1], (DEC_BATCH, DEC_SEQ, D_MODEL), 1.0),
        "state_conv": nrm(ks[2], (L, DEC_BATCH, CONV_DW_WIDTH - 1, CONV_CH), 0.5),
        "state_qkv_conv": nrm(ks[3], (L, DEC_BATCH, SHORT_CONV - 1, 3 * GDN_WIDTH), 1.0),
        "state_gdn": nrm(ks[4], (L, DEC_BATCH, GDN_HEADS, HEAD_DIM, HEAD_DIM), 0.1),
        "state_ffn_conv": nrm(ks[5], (L, DEC_BATCH, FFN_CONV - 1, 2 * FFN_DIM), 1.0),
        "g_pre_mix": 1.0 + nrm(ks[6], (L, D_MODEL), 0.05),
        "w_in": nrm(ks[7], (L, D_MODEL, IN_COLS), D_MODEL ** -0.5),
        "w_dw": nrm(ks[8], (L, CONV_DW_WIDTH, CONV_CH), CONV_DW_WIDTH ** -0.5),
        "b_dw": nrm(ks[9], (L, CONV_CH), 0.02),
        "gn_g": 1.0 + nrm(ks[12], (L, CONV_CH), 0.05),
        "gn_b": nrm(ks[13], (L, CONV_CH), 0.02),
        "w_sc": nrm(ks[14], (L, SHORT_CONV, 3 * GDN_WIDTH), SHORT_CONV ** -0.5),
        "a_log": jnp.log(a_val),
        "dt_bias": dt_bias,
        "onorm_g": 1.0 + nrm(ks[15], (L, HEAD_DIM), 0.05),
        "w_out": nrm(ks[16], (L, MIX_WIDTH, D_MODEL), MIX_WIDTH ** -0.5),
        "g_post_mix": 1.0 + nrm(ks[17], (L, D_MODEL), 0.05),
        "g_pre_ffn": 1.0 + nrm(ks[18], (L, D_MODEL), 0.05),
        "w_up": nrm(ks[19], (L, D_MODEL, 2 * FFN_DIM), D_MODEL ** -0.5),
        "w_ffn_dw": nrm(ks[20], (L, FFN_CONV, 2 * FFN_DIM), FFN_CONV ** -0.5),
        "b_ffn_dw": nrm(ks[21], (L, 2 * FFN_DIM), 0.02),
        "w_down": nrm(ks[22], (L, FFN_DIM, D_MODEL), FFN_DIM ** -0.5),
        "g_post_ffn": 1.0 + nrm(ks[23], (L, D_MODEL), 0.05),
    }


def reference(x_prompt, x_sample, state_conv, state_qkv_conv, state_gdn, state_ffn_conv,
              g_pre_mix, w_in, w_dw, b_dw, gn_g, gn_b, w_sc, a_log, dt_bias, onorm_g,
              w_out, g_post_mix, g_pre_ffn, w_up, w_ffn_dw, b_ffn_dw, w_down, g_post_ffn):
    xp = x_prompt
    xs = x_sample
    dt = x_prompt.dtype
    cp, qp, sp, fp = [], [], [], []
    cs, qs, ss, fs = [], [], [], []
    for l in range(DEPTH):
        params = (g_pre_mix[l], w_in[l], w_dw[l], b_dw[l], gn_g[l], gn_b[l], w_sc[l], a_log[l],
                  dt_bias[l], onorm_g[l], w_out[l], g_post_mix[l], g_pre_ffn[l], w_up[l],
                  w_ffn_dw[l], b_ffn_dw[l], w_down[l], g_post_ffn[l])
        xp, c_b, q_b, s_b, f_b = trunk_layer(
            xp,
            jnp.zeros((xp.shape[0], CONV_DW_WIDTH - 1, CONV_CH), dt),
            jnp.zeros((xp.shape[0], SHORT_CONV - 1, 3 * GDN_WIDTH), dt),
            jnp.zeros((xp.shape[0], GDN_HEADS, HEAD_DIM, HEAD_DIM), jnp.float32),
            jnp.zeros((xp.shape[0], FFN_CONV - 1, 2 * FFN_DIM), dt),
            CHUNK, *params)
        cp.append(c_b); qp.append(q_b); sp.append(s_b); fp.append(f_b)
        xs, c_b, q_b, s_b, f_b = trunk_layer(
            xs, state_conv[l], state_qkv_conv[l], state_gdn[l], state_ffn_conv[l],
            xs.shape[1], *params)
        cs.append(c_b); qs.append(q_b); ss.append(s_b); fs.append(f_b)
    return (xp, xs, jnp.stack(cp), jnp.stack(qp), jnp.stack(sp), jnp.stack(fp),
            jnp.stack(cs), jnp.stack(qs), jnp.stack(ss), jnp.stack(fs))
```

```python
import functools

import jax
import jax.numpy as jnp
from jax import lax
from jax.experimental import pallas as pl
from jax.experimental.pallas import tpu as pltpu

F32 = jnp.float32
BF16 = jnp.bfloat16

EPS = 1e-6
HEAD_DIM = 128
CONV_GROUP = 128
CONV_TAPS = 31
SHORT_TAPS = 4
FFN_TAPS = 3
CHUNK = 64
HIST = 32
SUBLANES = 8
LANES = 128
VMEM_LIMIT_BYTES = 56 * 1024 * 1024


def _cparams(*semantics):
    return pltpu.CompilerParams(dimension_semantics=semantics, vmem_limit_bytes=VMEM_LIMIT_BYTES)


def _sigmoid(x):
    return 1.0 / (1.0 + jnp.exp(-x))


def _softplus(x):
    return jnp.maximum(x, 0.0) + jnp.log(1.0 + jnp.exp(-jnp.abs(x)))


def _dot(a, b):
    return jnp.dot(a, b, preferred_element_type=F32)


def _dot_nt(a, b):
    return lax.dot_general(a, b, (((1,), (1,)), ((), ())), preferred_element_type=F32)


def _dot_tn(a, b):
    return lax.dot_general(a, b, (((0,), (0,)), ((), ())), preferred_element_type=F32)


def _rms(x, g):
    return x * lax.rsqrt(jnp.mean(x * x, axis=-1, keepdims=True) + EPS) * g


def _rmsnorm_kernel(x_ref, g_ref, h_ref):
    h_ref[...] = _rms(x_ref[...], g_ref[...]).astype(h_ref.dtype)


def _rmsnorm(x, g, tm):
    m, d = x.shape
    return pl.pallas_call(
        _rmsnorm_kernel,
        grid=(m // tm,),
        in_specs=[pl.BlockSpec((tm, d), lambda i: (i, 0)), pl.BlockSpec((1, d), lambda i: (0, 0))],
        out_specs=pl.BlockSpec((tm, d), lambda i: (i, 0)),
        out_shape=jax.ShapeDtypeStruct((m, d), BF16),
        compiler_params=_cparams("arbitrary"),
        name="rmsnorm",
    )(x, g)


def _conformer_kernel(*refs, rows, spt, tps, has_state, row_chunk):
    if has_state:
        (h_ref, wv_ref, wg_ref, wba_ref, wdw_ref, bdw_ref, gng_ref, gnb_ref, st_ref,
         a_ref, ba_ref, ns_ref, u_buf) = refs
    else:
        (h_ref, wv_ref, wg_ref, wba_ref, wdw_ref, bdw_ref, gng_ref, gnb_ref,
         a_ref, ba_ref, ns_ref, u_buf) = refs
        st_ref = None
    ch = wv_ref.shape[1]
    n_groups = ch // CONV_GROUP
    i = pl.program_id(0)

    h = h_ref[...]
    val = _dot(h, wv_ref[...])
    gate = _dot(h, wg_ref[...])
    ba_ref[...] = _dot(h, wba_ref[...])
    u = val * _sigmoid(gate)

    def init_history():
        for s in range(spt):
            u_buf[s, 0:HIST, :] = st_ref[s] if has_state else jnp.zeros((HIST, ch), F32)

    if tps > 1:
        pl.when(i % tps == 0)(init_history)
    else:
        init_history()
    for s in range(spt):
        u_buf[s, HIST:HIST + rows, :] = u[s * rows:(s + 1) * rows, :]

    n_rc = rows // row_chunk
    win = row_chunk + HIST
    lead = HIST - (CONV_TAPS - 1)

    def chunk_body(idx, carry):
        s = idx // n_rc
        t0 = pl.multiple_of((idx % n_rc) * row_chunk, row_chunk)
        for c in range(n_groups):
            lanes = slice(c * CONV_GROUP, (c + 1) * CONV_GROUP)
            x = u_buf[s, pl.ds(t0, win), lanes]
            acc = jnp.broadcast_to(bdw_ref[:, lanes], (row_chunk, CONV_GROUP))
            for shift in range(SUBLANES):
                xs = x if shift == 0 else pltpu.roll(x, win - shift, axis=0)
                for k in range(CONV_TAPS):
                    if (lead + k) % SUBLANES != shift:
                        continue
                    off = lead + k - shift
                    acc = acc + wdw_ref[k:k + 1, lanes] * xs[off:off + row_chunk, :]
            mu = jnp.mean(acc, axis=-1, keepdims=True)
            d = acc - mu
            var = jnp.mean(d * d, axis=-1, keepdims=True)
            un = d * lax.rsqrt(var + EPS) * gng_ref[:, lanes] + gnb_ref[:, lanes]
            r0 = pl.multiple_of(s * rows + t0, row_chunk)
            a_ref[pl.ds(r0, row_chunk), lanes] = (un * _sigmoid(un)).astype(a_ref.dtype)
        return carry

    lax.fori_loop(0, spt * n_rc, chunk_body, 0)

    for s in range(spt):
        tail = u_buf[s, rows:rows + HIST, :]
        ns_ref[s] = tail
        if tps > 1:
            u_buf[s, 0:HIST, :] = tail


def _conformer(h, wv, wg, wba, w_dw, b_dw, gn_g, gn_b, state, *, n_seq, seq_len, tm):
    m, d = h.shape
    ch = wv.shape[1]
    rows = min(tm, seq_len)
    spt = tm // rows
    tps = seq_len // rows
    assert m == n_seq * seq_len and m % tm == 0 and rows * spt == tm and rows * tps == seq_len
    assert spt == 1 or tps == 1
    row_chunk = min(64, rows)
    seq_block = (lambda i: (i // tps, 0, 0)) if tps > 1 else (lambda i: (i, 0, 0))
    const = lambda i: (0, 0)
    in_specs = [
        pl.BlockSpec((tm, d), lambda i: (i, 0)),
        pl.BlockSpec((d, ch), const), pl.BlockSpec((d, ch), const), pl.BlockSpec((d, LANES), const),
        pl.BlockSpec((CONV_TAPS, ch), const), pl.BlockSpec((1, ch), const),
        pl.BlockSpec((1, ch), const), pl.BlockSpec((1, ch), const),
    ]
    args = [h, wv, wg, wba, w_dw, b_dw, gn_g, gn_b]
    if state is not None:
        in_specs.append(pl.BlockSpec((spt, HIST, ch), seq_block))
        args.append(state)
    kern = functools.partial(_conformer_kernel, rows=rows, spt=spt, tps=tps,
                             has_state=state is not None, row_chunk=row_chunk)
    return pl.pallas_call(
        kern,
        grid=(m // tm,),
        in_specs=in_specs,
        out_specs=[pl.BlockSpec((tm, ch), lambda i: (i, 0)),
                   pl.BlockSpec((tm, LANES), lambda i: (i, 0)),
                   pl.BlockSpec((spt, HIST, ch), seq_block)],
        out_shape=[jax.ShapeDtypeStruct((m, ch), BF16),
                   jax.ShapeDtypeStruct((m, LANES), F32),
                   jax.ShapeDtypeStruct((n_seq, HIST, ch), F32)],
        scratch_shapes=[pltpu.VMEM((spt, HIST + rows, ch), F32)],
        compiler_params=_cparams("arbitrary"),
        name="conformer",
    )(*args)


def _matmul_kernel(h_ref, w_ref, o_ref):
    o_ref[...] = _dot(h_ref[...], w_ref[...]).astype(o_ref.dtype)


def _matmul(h, w, *, tm, tn):
    m, k = h.shape
    n = w.shape[1]
    assert m % tm == 0 and n % tn == 0
    return pl.pallas_call(
        _matmul_kernel,
        grid=(n // tn, m // tm),
        in_specs=[pl.BlockSpec((tm, k), lambda j, i: (i, 0)), pl.BlockSpec((k, tn), lambda j, i: (0, j))],
        out_specs=pl.BlockSpec((tm, tn), lambda j, i: (i, j)),
        out_shape=jax.ShapeDtypeStruct((m, n), F32),
        compiler_params=_cparams("arbitrary", "arbitrary"),
        name="qkvz_proj",
    )(h, w)


def _gdn_kernel(*refs, n_heads, n_chunks, has_state):
    if has_state:
        (q_ref, k_ref, v_ref, z_ref, ba_ref, wsc_ref, alog_ref, dtb_ref, on_ref, st_ref, s0_ref,
         o_ref, nst_ref, sfin_ref, s_buf, hist) = refs
    else:
        (q_ref, k_ref, v_ref, z_ref, ba_ref, wsc_ref, alog_ref, dtb_ref, on_ref,
         o_ref, nst_ref, sfin_ref, s_buf, hist) = refs
        st_ref = s0_ref = None
    L = CHUNK
    gw = n_heads * HEAD_DIM
    n = pl.program_id(1)

    @pl.when(n == 0)
    def _():
        if has_state:
            s_buf[...] = s0_ref[0]
            for p in range(3):
                hist[p] = st_ref[0, :, p * gw:(p + 1) * gw]
        else:
            s_buf[...] = jnp.zeros(s_buf.shape, F32)
            hist[...] = jnp.zeros(hist.shape, F32)

    conv = []
    for p, x_ref in enumerate((q_ref, k_ref, v_ref)):
        x = x_ref[...]
        xe = jnp.concatenate([hist[p], x], axis=0)
        w = wsc_ref[:, p * gw:(p + 1) * gw]
        y = w[SHORT_TAPS - 1:SHORT_TAPS, :] * x
        for j in range(1, SHORT_TAPS):
            xj = pltpu.roll(xe, j, axis=0)[SUBLANES:SUBLANES + L, :]
            y = y + w[SHORT_TAPS - 1 - j:SHORT_TAPS - j, :] * xj
        conv.append(y * _sigmoid(y))
        tail = x[L - SUBLANES:L, :]
        hist[p] = tail
        nst_ref[0, :, p * gw:(p + 1) * gw] = tail
    qc, kc, vc = conv

    ba = ba_ref[...]
    beta_all = _sigmoid(ba)
    g_all = -jnp.exp(alog_ref[...]) * _softplus(ba + dtb_ref[...])
    row = lax.broadcasted_iota(jnp.int32, (L, LANES), 0)
    lane = lax.broadcasted_iota(jnp.int32, (L, LANES), 1)
    gcum = g_all
    d = 1
    while d < L:
        gcum = gcum + jnp.where(row >= d, pltpu.roll(gcum, d, axis=0), 0.0)
        d *= 2
    g_last = gcum[L - 1:L, :]
    eg_all = jnp.exp(gcum)
    ekd_all = jnp.exp(g_last - gcum)
    gl_all = jnp.exp(g_last)
    gcum_t = jnp.concatenate([gcum, jnp.zeros((LANES - L, LANES), F32)], axis=0).T

    incl = (lane <= row)
    strict = (lane < row)
    eye_hi = (lane == row + L).astype(F32)
    right = lane >= L
    zeros_k = jnp.zeros((L, HEAD_DIM), BF16)

    for hd in range(n_heads):
        lanes = slice(hd * HEAD_DIM, (hd + 1) * HEAD_DIM)
        cb, ca = hd, n_heads + hd
        beta = beta_all[:, cb:cb + 1]
        gcol = gcum[:, ca:ca + 1]
        grow = gcum_t[ca:ca + 1, :]
        eg = eg_all[:, ca:ca + 1]
        ekd = ekd_all[:, ca:ca + 1]
        gl = gl_all[:, ca:ca + 1]

        q = qc[:, lanes]
        k = kc[:, lanes]
        v = vc[:, lanes]
        qn = q * lax.rsqrt(jnp.sum(q * q, axis=-1, keepdims=True) + EPS) * (HEAD_DIM ** -0.5)
        kn = k * lax.rsqrt(jnp.sum(k * k, axis=-1, keepdims=True) + EPS)
        kb = kn * beta
        k_bf = kn.astype(BF16)

        a_raw = _dot_nt(jnp.concatenate([kb, qn], axis=0).astype(BF16),
                        jnp.concatenate([k_bf, zeros_k], axis=0))
        decay = jnp.exp(jnp.where(incl, gcol - grow, -jnp.inf))
        qk = a_raw[L:2 * L, :] * decay
        x = jnp.where(strict, -(a_raw[0:L, :] * decay), 0.0) + eye_hi
        for _ in range(5):
            x = _dot(x[:, 0:L].astype(BF16), x.astype(BF16)) + jnp.where(right, x, 0.0)
        xr = pltpu.roll(x, L, axis=1)
        t = _dot(x[:, 0:L].astype(BF16), xr.astype(BF16)) + xr

        rhs = jnp.concatenate([v * beta, kb * eg], axis=1).astype(BF16)
        uw = _dot(t[:, 0:L].astype(BF16), rhs)
        u = uw[:, 0:HEAD_DIM]
        w = uw[:, HEAD_DIM:2 * HEAD_DIM]

        s = s_buf[hd]
        m1 = _dot(jnp.concatenate([w, qn * eg], axis=0).astype(BF16), s.astype(BF16))
        v_new = u - m1[0:L, :]
        vn_bf = v_new.astype(BF16)
        o = m1[L:2 * L, :] + _dot(qk[:, 0:L].astype(BF16), vn_bf)
        s_buf[hd] = s * gl + _dot_tn((kn * ekd).astype(BF16), vn_bf)

        o = o * lax.rsqrt(jnp.mean(o * o, axis=-1, keepdims=True) + EPS) * on_ref[...]
        zz = z_ref[:, lanes]
        o_ref[:, lanes] = (o * (zz * _sigmoid(zz))).astype(o_ref.dtype)

    @pl.when(n == n_chunks - 1)
    def _():
        sfin_ref[0] = s_buf[...]


def _gdn(qkvz, ba, w_sc, alog_row, dtb_row, onorm_g, st_qkv, s0, *, n_seq, seq_len, n_heads):
    m = qkvz.shape[0]
    gw = n_heads * HEAD_DIM
    n_chunks = seq_len // CHUNK
    assert m == n_seq * seq_len and n_chunks * CHUNK == seq_len
    rows = lambda b, n: b * n_chunks + n
    const = lambda b, n: (0, 0)
    in_specs = [pl.BlockSpec((CHUNK, gw), lambda b, n, p=p: (rows(b, n), p)) for p in range(4)]
    in_specs += [
        pl.BlockSpec((CHUNK, LANES), lambda b, n: (rows(b, n), 0)),
        pl.BlockSpec((SHORT_TAPS, 3 * gw), const),
        pl.BlockSpec((1, LANES), const), pl.BlockSpec((1, LANES), const), pl.BlockSpec((1, HEAD_DIM), const),
    ]
    args = [qkvz, qkvz, qkvz, qkvz, ba, w_sc, alog_row, dtb_row, onorm_g]
    has_state = st_qkv is not None
    if has_state:
        in_specs += [pl.BlockSpec((1, SUBLANES, 3 * gw), lambda b, n: (b, 0, 0)),
                     pl.BlockSpec((1, n_heads, HEAD_DIM, HEAD_DIM), lambda b, n: (b, 0, 0, 0))]
        args += [st_qkv, s0]
    kern = functools.partial(_gdn_kernel, n_heads=n_heads, n_chunks=n_chunks, has_state=has_state)
    return pl.pallas_call(
        kern,
        grid=(n_seq, n_chunks),
        in_specs=in_specs,
        out_specs=[pl.BlockSpec((CHUNK, gw), lambda b, n: (rows(b, n), 0)),
                   pl.BlockSpec((1, SUBLANES, 3 * gw), lambda b, n: (b, 0, 0)),
                   pl.BlockSpec((1, n_heads, HEAD_DIM, HEAD_DIM), lambda b, n: (b, 0, 0, 0))],
        out_shape=[jax.ShapeDtypeStruct((m, gw), BF16),
                   jax.ShapeDtypeStruct((n_seq, SUBLANES, 3 * gw), F32),
                   jax.ShapeDtypeStruct((n_seq, n_heads, HEAD_DIM, HEAD_DIM), F32)],
        scratch_shapes=[pltpu.VMEM((n_heads, HEAD_DIM, HEAD_DIM), F32),
                        pltpu.VMEM((3, SUBLANES, gw), F32)],
        compiler_params=_cparams("arbitrary", "arbitrary"),
        name="gdn",
    )(*args)


def _outproj_kernel(a_ref, b_ref, wa_ref, wb_ref, x_ref, gpost_ref, gnext_ref, xo_ref, ho_ref):
    y = _dot(a_ref[...], wa_ref[...]) + _dot(b_ref[...], wb_ref[...])
    xn = x_ref[...] + _rms(y, gpost_ref[...])
    xo_ref[...] = xn
    ho_ref[...] = _rms(xn, gnext_ref[...]).astype(ho_ref.dtype)


def _outproj(a, b, w_out, x, g_post, g_next, *, tm):
    m, ka = a.shape
    kb = b.shape[1]
    d = w_out.shape[1]
    assert w_out.shape[0] == ka + kb and ka == kb and m % tm == 0
    row = lambda i: (i, 0)
    const = lambda i: (0, 0)
    return pl.pallas_call(
        _outproj_kernel,
        grid=(m // tm,),
        in_specs=[pl.BlockSpec((tm, ka), row), pl.BlockSpec((tm, kb), row),
                  pl.BlockSpec((ka, d), lambda i: (0, 0)), pl.BlockSpec((kb, d), lambda i: (1, 0)),
                  pl.BlockSpec((tm, d), row), pl.BlockSpec((1, d), const), pl.BlockSpec((1, d), const)],
        out_specs=[pl.BlockSpec((tm, d), row), pl.BlockSpec((tm, d), row)],
        out_shape=[jax.ShapeDtypeStruct((m, d), F32), jax.ShapeDtypeStruct((m, d), BF16)],
        compiler_params=_cparams("arbitrary"),
        name="outproj",
    )(a, b, w_out, w_out, x, g_post, g_next)


def _ffn_up_kernel(*refs, rows, spt, tps, has_state):
    if has_state:
        (h_ref, wg_ref, wu_ref, cwg_ref, cwu_ref, cbg_ref, cbu_ref, stg_ref, stu_ref,
         act_ref, nsg_ref, nsu_ref, carry) = refs
    else:
        (h_ref, wg_ref, wu_ref, cwg_ref, cwu_ref, cbg_ref, cbu_ref,
         act_ref, nsg_ref, nsu_ref, carry) = refs
        stg_ref = stu_ref = None
    tm = rows * spt
    tn = wg_ref.shape[1]
    i = pl.program_id(1)
    h = h_ref[...]
    assert rows & (rows - 1) == 0
    pos = lax.broadcasted_iota(jnp.int32, (tm, tn), 0) & (rows - 1)

    if tps > 1:
        @pl.when(i % tps == 0)
        def _():
            carry[...] = jnp.zeros(carry.shape, F32)

    outs = []
    for part, (w_ref, cw_ref, cb_ref, st_ref, ns_ref) in enumerate((
            (wg_ref, cwg_ref, cbg_ref, stg_ref, nsg_ref), (wu_ref, cwu_ref, cbu_ref, stu_ref, nsu_ref))):
        g = _dot(h, w_ref[...])
        prev1, prev2 = [], []
        for s in range(spt):
            if has_state:
                hrows = st_ref[s]
            elif tps > 1:
                hrows = carry[part]
            else:
                hrows = jnp.zeros((SUBLANES, tn), F32)
            prev1.append(jnp.broadcast_to(hrows[SUBLANES - 1:SUBLANES, :], (rows, tn)))
            prev2.append(jnp.broadcast_to(hrows[SUBLANES - 2:SUBLANES - 1, :], (rows, tn)))
        prev1 = prev1[0] if spt == 1 else jnp.concatenate(prev1, axis=0)
        prev2 = prev2[0] if spt == 1 else jnp.concatenate(prev2, axis=0)
        g1 = jnp.where(pos == 0, prev1, pltpu.roll(g, 1, axis=0))
        g2 = jnp.where(pos == 0, prev2, jnp.where(pos == 1, prev1, pltpu.roll(g, 2, axis=0)))
        cw = cw_ref[...]
        outs.append(cb_ref[...] + cw[2:3, :] * g + cw[1:2, :] * g1 + cw[0:1, :] * g2)
        for s in range(spt):
            tail = g[(s + 1) * rows - SUBLANES:(s + 1) * rows, :]
            ns_ref[s] = tail
        if tps > 1:
            carry[part] = g[tm - SUBLANES:tm, :]
    yg, yu = outs
    act_ref[...] = (yg * _sigmoid(yg) * yu).astype(act_ref.dtype)


def _ffn_up(h, w_up, cw, cb, state, *, n_seq, seq_len, tm, tn):
    m, d = h.shape
    ffn = w_up.shape[1] // 2
    nt = ffn // tn
    rows = min(tm, seq_len)
    spt = tm // rows
    tps = seq_len // rows
    assert m == n_seq * seq_len and m % tm == 0 and nt * tn == ffn and (spt == 1 or tps == 1)
    has_state = state is not None
    seq_idx = (lambda i: i // tps) if tps > 1 else (lambda i: i)
    in_specs = [
        pl.BlockSpec((tm, d), lambda j, i: (i, 0)),
        pl.BlockSpec((d, tn), lambda j, i: (0, j)), pl.BlockSpec((d, tn), lambda j, i: (0, j + nt)),
        pl.BlockSpec((FFN_TAPS, tn), lambda j, i: (0, j)), pl.BlockSpec((FFN_TAPS, tn), lambda j, i: (0, j + nt)),
        pl.BlockSpec((1, tn), lambda j, i: (0, j)), pl.BlockSpec((1, tn), lambda j, i: (0, j + nt)),
    ]
    args = [h, w_up, w_up, cw, cw, cb, cb]
    if has_state:
        in_specs += [pl.BlockSpec((spt, SUBLANES, tn), lambda j, i: (seq_idx(i), 0, j)),
                     pl.BlockSpec((spt, SUBLANES, tn), lambda j, i: (seq_idx(i), 0, j + nt))]
        args += [state, state]
    kern = functools.partial(_ffn_up_kernel, rows=rows, spt=spt, tps=tps, has_state=has_state)
    ns_spec = pl.BlockSpec((spt, SUBLANES, tn), lambda j, i: (seq_idx(i), 0, j))
    return pl.pallas_call(
        kern,
        grid=(nt, m // tm),
        in_specs=in_specs,
        out_specs=[pl.BlockSpec((tm, tn), lambda j, i: (i, j)), ns_spec, ns_spec],
        out_shape=[jax.ShapeDtypeStruct((m, ffn), BF16),
                   jax.ShapeDtypeStruct((n_seq, SUBLANES, ffn), F32),
                   jax.ShapeDtypeStruct((n_seq, SUBLANES, ffn), F32)],
        scratch_shapes=[pltpu.VMEM((2, SUBLANES, tn), F32)],
        compiler_params=_cparams("arbitrary", "arbitrary"),
        name="ffn_up",
    )(*args)


def _ffn_down_kernel(act_ref, w_ref, x_ref, gpost_ref, gnext_ref, xo_ref, ho_ref, acc):
    kk = pl.program_id(1)

    @pl.when(kk == 0)
    def _():
        acc[...] = jnp.zeros(acc.shape, F32)

    acc[...] += _dot(act_ref[...], w_ref[...])

    @pl.when(kk == pl.num_programs(1) - 1)
    def _():
        xn = x_ref[...] + _rms(acc[...], gpost_ref[...])
        xo_ref[...] = xn
        ho_ref[...] = _rms(xn, gnext_ref[...]).astype(ho_ref.dtype)


def _ffn_down(act, w_down, x, g_post, g_next, *, tm, tk):
    m, k = act.shape
    d = w_down.shape[1]
    assert m % tm == 0 and k % tk == 0
    row = lambda i, kk: (i, 0)
    const = lambda i, kk: (0, 0)
    return pl.pallas_call(
        _ffn_down_kernel,
        grid=(m // tm, k // tk),
        in_specs=[pl.BlockSpec((tm, tk), lambda i, kk: (i, kk)), pl.BlockSpec((tk, d), lambda i, kk: (kk, 0)),
                  pl.BlockSpec((tm, d), row), pl.BlockSpec((1, d), const), pl.BlockSpec((1, d), const)],
        out_specs=[pl.BlockSpec((tm, d), row), pl.BlockSpec((tm, d), row)],
        out_shape=[jax.ShapeDtypeStruct((m, d), F32), jax.ShapeDtypeStruct((m, d), BF16)],
        scratch_shapes=[pltpu.VMEM((tm, d), F32)],
        compiler_params=_cparams("arbitrary", "arbitrary"),
        name="ffn_down",
    )(act, w_down, x, g_post, g_next)


def _pad_front(state, rows):
    return jnp.pad(state, ((0, 0), (rows - state.shape[1], 0), (0, 0)))


def _layer(x, h, states, p, g_next, *, n_seq, seq_len, tm):
    n_heads = p["a_log"].shape[-1]
    gw = n_heads * HEAD_DIM
    if states is None:
        st_conv = st_qkv = st_gdn = st_ffn = None
    else:
        st_conv = _pad_front(states[0], HIST)
        st_qkv = _pad_front(states[1], SUBLANES)
        st_gdn = states[2]
        st_ffn = _pad_front(states[3], SUBLANES)

    a_out, ba, ns_conv = _conformer(h, p["wv"], p["wg"], p["wba"], p["w_dw"], p["b_dw"], p["gn_g"], p["gn_b"],
                                    st_conv, n_seq=n_seq, seq_len=seq_len, tm=tm)
    qkvz = _matmul(h, p["wqkvz"], tm=tm, tn=512)
    b_out, ns_qkv, ns_gdn = _gdn(qkvz, ba, p["w_sc"], p["alog_row"], p["dtb_row"], p["onorm_g"], st_qkv, st_gdn,
                                 n_seq=n_seq, seq_len=seq_len, n_heads=n_heads)
    x, h = _outproj(a_out, b_out, p["w_out"], x, p["g_post_mix"], p["g_pre_ffn"], tm=tm)
    act, ns_fg, ns_fu = _ffn_up(h, p["w_up"], p["w_ffn_dw"], p["b_ffn_dw"], st_ffn,
                                n_seq=n_seq, seq_len=seq_len, tm=tm, tn=512)
    x, h = _ffn_down(act, p["w_down"], x, p["g_post_ffn"], g_next, tm=tm, tk=1408)
    new_states = (ns_conv[:, HIST - (CONV_TAPS - 1):, :],
                  ns_qkv[:, SUBLANES - (SHORT_TAPS - 1):, :],
                  ns_gdn,
                  jnp.concatenate([ns_fg, ns_fu], axis=-1)[:, SUBLANES - (FFN_TAPS - 1):, :])
    return x, h, new_states


def _layer_params(l, g_pre_mix, w_in, w_dw, b_dw, gn_g, gn_b, w_sc, a_log, dt_bias, onorm_g,
                  w_out, g_post_mix, g_pre_ffn, w_up, w_ffn_dw, b_ffn_dw, w_down, g_post_ffn):
    ch = w_dw.shape[-1]
    n_heads = a_log.shape[-1]
    gw = n_heads * HEAD_DIM
    c1 = 2 * ch
    c3 = c1 + 4 * gw
    wl = w_in[l]
    wba = jnp.pad(wl[:, c3:c3 + 2 * n_heads], ((0, 0), (0, LANES - 2 * n_heads)))
    row = lambda v: v[l][None, :]
    gate_cols = lambda v: jnp.pad(v[l], (n_heads, LANES - 2 * n_heads))[None, :]
    return dict(
        wv=wl[:, :ch].astype(BF16), wg=wl[:, ch:c1].astype(BF16), wqkvz=wl[:, c1:c3].astype(BF16),
        wba=wba.astype(BF16),
        w_dw=w_dw[l], b_dw=row(b_dw), gn_g=row(gn_g), gn_b=row(gn_b),
        w_sc=w_sc[l], alog_row=gate_cols(a_log), dtb_row=gate_cols(dt_bias), onorm_g=row(onorm_g),
        a_log=a_log[l],
        w_out=w_out[l].astype(BF16), g_post_mix=row(g_post_mix), g_pre_ffn=row(g_pre_ffn),
        w_up=w_up[l].astype(BF16), w_ffn_dw=w_ffn_dw[l], b_ffn_dw=row(b_ffn_dw),
        w_down=w_down[l].astype(BF16), g_post_ffn=row(g_post_ffn),
    )


def kernel(x_prompt, x_sample, state_conv, state_qkv_conv, state_gdn, state_ffn_conv, g_pre_mix, w_in, w_dw, b_dw, gn_g, gn_b, w_sc, a_log, dt_bias, onorm_g, w_out, g_post_mix, g_pre_ffn, w_up, w_ffn_dw, b_ffn_dw, w_down, g_post_ffn):
    depth = w_in.shape[0]
    bp, tp, d = x_prompt.shape
    bs, ts, _ = x_sample.shape
    tm = 512
    xp = x_prompt.reshape(bp * tp, d)
    xs = x_sample.reshape(bs * ts, d)
    hp = _rmsnorm(xp, g_pre_mix[0][None, :], tm)
    hs = _rmsnorm(xs, g_pre_mix[0][None, :], tm)
    outs_p, outs_s = [], []
    for l in range(depth):
        p = _layer_params(l, g_pre_mix, w_in, w_dw, b_dw, gn_g, gn_b, w_sc, a_log, dt_bias, onorm_g,
                          w_out, g_post_mix, g_pre_ffn, w_up, w_ffn_dw, b_ffn_dw, w_down, g_post_ffn)
        g_next = g_pre_mix[(l + 1) % depth][None, :]
        xp, hp, st_p = _layer(xp, hp, None, p, g_next, n_seq=bp, seq_len=tp, tm=tm)
        xs, hs, st_s = _layer(xs, hs, (state_conv[l], state_qkv_conv[l], state_gdn[l], state_ffn_conv[l]),
                              p, g_next, n_seq=bs, seq_len=ts, tm=tm)
        outs_p.append(st_p)
        outs_s.append(st_s)
    stack = lambda outs, j: jnp.stack([o[j] for o in outs])
    return (xp.reshape(bp, tp, d), xs.reshape(bs, ts, d),
            stack(outs_p, 0), stack(outs_p, 1), stack(outs_p, 2), stack(outs_p, 3),
            stack(outs_s, 0), stack(outs_s, 1), stack(outs_s, 2), stack(outs_s, 3))
```

```python
import functools

import jax
import jax.numpy as jnp
from jax import lax
from jax.experimental import pallas as pl
from jax.experimental.pallas import tpu as pltpu

F32 = jnp.float32
BF16 = jnp.bfloat16

EPS = 1e-6
HEAD_DIM = 128
CONV_GROUP = 128
CONV_TAPS = 31
SHORT_TAPS = 4
FFN_TAPS = 3
CHUNK = 64
HIST = 32
SUBLANES = 8
LANES = 128
VMEM_LIMIT_BYTES = 56 * 1024 * 1024


def _cparams(*semantics):
    return pltpu.CompilerParams(dimension_semantics=semantics, vmem_limit_bytes=VMEM_LIMIT_BYTES)


def _sigmoid(x):
    return 1.0 / (1.0 + jnp.exp(-x))


def _softplus(x):
    return jnp.maximum(x, 0.0) + jnp.log(1.0 + jnp.exp(-jnp.abs(x)))


def _dot(a, b):
    return jnp.dot(a, b, preferred_element_type=F32)


def _dot_nt(a, b):
    return lax.dot_general(a, b, (((1,), (1,)), ((), ())), preferred_element_type=F32)


def _dot_tn(a, b):
    return lax.dot_general(a, b, (((0,), (0,)), ((), ())), preferred_element_type=F32)


def _rms(x, g):
    return x * lax.rsqrt(jnp.mean(x * x, axis=-1, keepdims=True) + EPS) * g


def _rmsnorm_kernel(x_ref, g_ref, h_ref):
    h_ref[...] = _rms(x_ref[...], g_ref[...]).astype(h_ref.dtype)


def _rmsnorm(x, g, tm):
    m, d = x.shape
    return pl.pallas_call(
        _rmsnorm_kernel,
        grid=(m // tm,),
        in_specs=[pl.BlockSpec((tm, d), lambda i: (i, 0)), pl.BlockSpec((1, d), lambda i: (0, 0))],
        out_specs=pl.BlockSpec((tm, d), lambda i: (i, 0)),
        out_shape=jax.ShapeDtypeStruct((m, d), BF16),
        compiler_params=_cparams("arbitrary"),
        name="rmsnorm",
    )(x, g)


def _conformer_kernel(*refs, rows, spt, tps, has_state, row_chunk):
    if has_state:
        (h_ref, wv_ref, wg_ref, wba_ref, wdw_ref, bdw_ref, gng_ref, gnb_ref, st_ref,
         a_ref, ba_ref, ns_ref, u_buf) = refs
    else:
        (h_ref, wv_ref, wg_ref, wba_ref, wdw_ref, bdw_ref, gng_ref, gnb_ref,
         a_ref, ba_ref, ns_ref, u_buf) = refs
        st_ref = None
    ch = wv_ref.shape[1]
    n_groups = ch // CONV_GROUP
    i = pl.program_id(0)

    h = h_ref[...]
    val = _dot(h, wv_ref[...])
    gate = _dot(h, wg_ref[...])
    ba_ref[...] = _dot(h, wba_ref[...])
    u = val * _sigmoid(gate)

    def init_history():
        for s in range(spt):
            u_buf[s, 0:HIST, :] = st_ref[s] if has_state else jnp.zeros((HIST, ch), F32)

    if tps > 1:
        pl.when(i % tps == 0)(init_history)
    else:
        init_history()
    for s in range(spt):
        u_buf[s, HIST:HIST + rows, :] = u[s * rows:(s + 1) * rows, :]

    n_rc = rows // row_chunk
    win = row_chunk + HIST
    lead = HIST - (CONV_TAPS - 1)

    def chunk_body(idx, carry):
        s = idx // n_rc
        t0 = pl.multiple_of((idx % n_rc) * row_chunk, row_chunk)
        for c in range(n_groups):
            lanes = slice(c * CONV_GROUP, (c + 1) * CONV_GROUP)
            x = u_buf[s, pl.ds(t0, win), lanes]
            acc = jnp.broadcast_to(bdw_ref[:, lanes], (row_chunk, CONV_GROUP))
            for shift in range(SUBLANES):
                xs = x if shift == 0 else pltpu.roll(x, win - shift, axis=0)
                for k in range(CONV_TAPS):
                    if (lead + k) % SUBLANES != shift:
                        continue
                    off = lead + k - shift
                    acc = acc + wdw_ref[k:k + 1, lanes] * xs[off:off + row_chunk, :]
            mu = jnp.mean(acc, axis=-1, keepdims=True)
            d = acc - mu
            var = jnp.mean(d * d, axis=-1, keepdims=True)
            un = d * lax.rsqrt(var + EPS) * gng_ref[:, lanes] + gnb_ref[:, lanes]
            r0 = pl.multiple_of(s * rows + t0, row_chunk)
            a_ref[pl.ds(r0, row_chunk), lanes] = (un * _sigmoid(un)).astype(a_ref.dtype)
        return carry

    lax.fori_loop(0, spt * n_rc, chunk_body, 0)

    for s in range(spt):
        tail = u_buf[s, rows:rows + HIST, :]
        ns_ref[s] = tail
        if tps > 1:
            u_buf[s, 0:HIST, :] = tail


def _conformer(h, wv, wg, wba, w_dw, b_dw, gn_g, gn_b, state, *, n_seq, seq_len, tm):
    m, d = h.shape
    ch = wv.shape[1]
    rows = min(tm, seq_len)
    spt = tm // rows
    tps = seq_len // rows
    assert m == n_seq * seq_len and m % tm == 0 and rows * spt == tm and rows * tps == seq_len
    assert spt == 1 or tps == 1
    row_chunk = min(64, rows)
    seq_block = (lambda i: (i // tps, 0, 0)) if tps > 1 else (lambda i: (i, 0, 0))
    const = lambda i: (0, 0)
    in_specs = [
        pl.BlockSpec((tm, d), lambda i: (i, 0)),
        pl.BlockSpec((d, ch), const), pl.BlockSpec((d, ch), const), pl.BlockSpec((d, LANES), const),
        pl.BlockSpec((CONV_TAPS, ch), const), pl.BlockSpec((1, ch), const),
        pl.BlockSpec((1, ch), const), pl.BlockSpec((1, ch), const),
    ]
    args = [h, wv, wg, wba, w_dw, b_dw, gn_g, gn_b]
    if state is not None:
        in_specs.append(pl.BlockSpec((spt, HIST, ch), seq_block))
        args.append(state)
    kern = functools.partial(_conformer_kernel, rows=rows, spt=spt, tps=tps,
                             has_state=state is not None, row_chunk=row_chunk)
    return pl.pallas_call(
        kern,
        grid=(m // tm,),
        in_specs=in_specs,
        out_specs=[pl.BlockSpec((tm, ch), lambda i: (i, 0)),
                   pl.BlockSpec((tm, LANES), lambda i: (i, 0)),
                   pl.BlockSpec((spt, HIST, ch), seq_block)],
        out_shape=[jax.ShapeDtypeStruct((m, ch), BF16),
                   jax.ShapeDtypeStruct((m, LANES), F32),
                   jax.ShapeDtypeStruct((n_seq, HIST, ch), F32)],
        scratch_shapes=[pltpu.VMEM((spt, HIST + rows, ch), F32)],
        compiler_params=_cparams("arbitrary"),
        name="conformer",
    )(*args)


def _matmul_kernel(h_ref, w_ref, o_ref):
    o_ref[...] = _dot(h_ref[...], w_ref[...]).astype(o_ref.dtype)


def _matmul(h, w, *, tm, tn):
    m, k = h.shape
    n = w.shape[1]
    assert m % tm == 0 and n % tn == 0
    return pl.pallas_call(
        _matmul_kernel,
        grid=(n // tn, m // tm),
        in_specs=[pl.BlockSpec((tm, k), lambda j, i: (i, 0)), pl.BlockSpec((k, tn), lambda j, i: (0, j))],
        out_specs=pl.BlockSpec((tm, tn), lambda j, i: (i, j)),
        out_shape=jax.ShapeDtypeStruct((m, n), F32),
        compiler_params=_cparams("arbitrary", "arbitrary"),
        name="qkvz_proj",
    )(h, w)


def _gdn_kernel(*refs, n_heads, n_chunks, sp, has_state):
    if has_state:
        (q_ref, k_ref, v_ref, z_ref, ba_ref, wsc_ref, alog_ref, dtb_ref, on_ref, st_ref, s0_ref,
         o_ref, nst_ref, sfin_ref, s_buf, hist) = refs
    else:
        (q_ref, k_ref, v_ref, z_ref, ba_ref, wsc_ref, alog_ref, dtb_ref, on_ref,
         o_ref, nst_ref, sfin_ref, s_buf, hist) = refs
        st_ref = s0_ref = None
    L = CHUNK
    gw = n_heads * HEAD_DIM
    n = pl.program_id(1)

    @pl.when(n == 0)
    def _():
        if has_state:
            s_buf[...] = s0_ref[...]
            for p in range(3):
                hist[:, p] = st_ref[:, :, p * gw:(p + 1) * gw]
        else:
            s_buf[...] = jnp.zeros(s_buf.shape, F32)
            hist[...] = jnp.zeros(hist.shape, F32)

    row = lax.broadcasted_iota(jnp.int32, (L, LANES), 0)
    lane = lax.broadcasted_iota(jnp.int32, (L, LANES), 1)
    incl = (lane <= row)
    strict = (lane < row)
    eye_hi = (lane == row + L).astype(F32)
    right = lane >= L
    zeros_k = jnp.zeros((L, HEAD_DIM), BF16)

    def sequence_prologue(s):
        conv = []
        for p, x_ref in enumerate((q_ref, k_ref, v_ref)):
            x = x_ref[s]
            xe = jnp.concatenate([hist[s, p], x], axis=0)
            w = wsc_ref[:, p * gw:(p + 1) * gw]
            y = w[SHORT_TAPS - 1:SHORT_TAPS, :] * x
            for j in range(1, SHORT_TAPS):
                xj = pltpu.roll(xe, j, axis=0)[SUBLANES:SUBLANES + L, :]
                y = y + w[SHORT_TAPS - 1 - j:SHORT_TAPS - j, :] * xj
            conv.append(y * _sigmoid(y))
            tail = x[L - SUBLANES:L, :]
            hist[s, p] = tail
            nst_ref[s, :, p * gw:(p + 1) * gw] = tail
        ba = ba_ref[s]
        beta_all = _sigmoid(ba)
        gcum = -jnp.exp(alog_ref[...]) * _softplus(ba + dtb_ref[...])
        d = 1
        while d < L:
            gcum = gcum + jnp.where(row >= d, pltpu.roll(gcum, d, axis=0), 0.0)
            d *= 2
        g_last = gcum[L - 1:L, :]
        gates = dict(beta=beta_all, gcum=gcum, eg=jnp.exp(gcum), ekd=jnp.exp(g_last - gcum), gl=jnp.exp(g_last),
                     gcum_t=jnp.concatenate([gcum, jnp.zeros((LANES - L, LANES), F32)], axis=0).T)
        return conv, gates

    def head_chain(s, hd, conv, gates):
        qc, kc, vc = conv
        lanes = slice(hd * HEAD_DIM, (hd + 1) * HEAD_DIM)
        cb, ca = hd, n_heads + hd
        beta = gates["beta"][:, cb:cb + 1]
        gcol = gates["gcum"][:, ca:ca + 1]
        grow = gates["gcum_t"][ca:ca + 1, :]
        eg = gates["eg"][:, ca:ca + 1]
        ekd = gates["ekd"][:, ca:ca + 1]
        gl = gates["gl"][:, ca:ca + 1]

        q = qc[:, lanes]
        k = kc[:, lanes]
        v = vc[:, lanes]
        qn = q * lax.rsqrt(jnp.sum(q * q, axis=-1, keepdims=True) + EPS) * (HEAD_DIM ** -0.5)
        kn = k * lax.rsqrt(jnp.sum(k * k, axis=-1, keepdims=True) + EPS)
        kb = kn * beta
        a_raw = _dot_nt(jnp.concatenate([kb, qn], axis=0).astype(BF16),
                        jnp.concatenate([kn.astype(BF16), zeros_k], axis=0))
        yield
        decay = jnp.exp(jnp.where(incl, gcol - grow, -jnp.inf))
        qk = a_raw[L:2 * L, :] * decay
        x = jnp.where(strict, -(a_raw[0:L, :] * decay), 0.0) + eye_hi
        for _ in range(5):
            x = _dot(x[:, 0:L].astype(BF16), x.astype(BF16)) + jnp.where(right, x, 0.0)
            yield
        xr = pltpu.roll(x, L, axis=1)
        t = _dot(x[:, 0:L].astype(BF16), xr.astype(BF16)) + xr
        yield
        rhs = jnp.concatenate([v * beta, kb * eg], axis=1).astype(BF16)
        uw = _dot(t[:, 0:L].astype(BF16), rhs)
        yield
        u = uw[:, 0:HEAD_DIM]
        w = uw[:, HEAD_DIM:2 * HEAD_DIM]
        st = s_buf[s, hd]
        m1 = _dot(jnp.concatenate([w, qn * eg], axis=0).astype(BF16), st.astype(BF16))
        yield
        vn_bf = (u - m1[0:L, :]).astype(BF16)
        o = m1[L:2 * L, :] + _dot(qk[:, 0:L].astype(BF16), vn_bf)
        yield
        s_buf[s, hd] = st * gl + _dot_tn((kn * ekd).astype(BF16), vn_bf)
        o = o * lax.rsqrt(jnp.mean(o * o, axis=-1, keepdims=True) + EPS) * on_ref[...]
        zz = z_ref[s, :, lanes]
        o_ref[s, :, lanes] = (o * (zz * _sigmoid(zz))).astype(o_ref.dtype)

    chains = []
    for s in range(sp):
        conv, gates = sequence_prologue(s)
        chains += [head_chain(s, hd, conv, gates) for hd in range(n_heads)]
    while chains:
        alive = []
        for c in chains:
            try:
                next(c)
                alive.append(c)
            except StopIteration:
                pass
        chains = alive

    @pl.when(n == n_chunks - 1)
    def _():
        sfin_ref[...] = s_buf[...]


def _gdn(qkvz, ba, w_sc, alog_row, dtb_row, onorm_g, st_qkv, s0, *, n_seq, seq_len, n_heads, sp):
    m = qkvz.shape[0]
    gw = n_heads * HEAD_DIM
    n_chunks = seq_len // CHUNK
    assert m == n_seq * seq_len and n_chunks * CHUNK == seq_len and n_seq % sp == 0
    qkvz3 = qkvz.reshape(n_seq, seq_len, 4 * gw)
    ba3 = ba.reshape(n_seq, seq_len, LANES)
    const = lambda b, n: (0, 0)
    in_specs = [pl.BlockSpec((sp, CHUNK, gw), lambda b, n, p=p: (b, n, p)) for p in range(4)]
    in_specs += [
        pl.BlockSpec((sp, CHUNK, LANES), lambda b, n: (b, n, 0)),
        pl.BlockSpec((SHORT_TAPS, 3 * gw), const),
        pl.BlockSpec((1, LANES), const), pl.BlockSpec((1, LANES), const), pl.BlockSpec((1, HEAD_DIM), const),
    ]
    args = [qkvz3, qkvz3, qkvz3, qkvz3, ba3, w_sc, alog_row, dtb_row, onorm_g]
    has_state = st_qkv is not None
    if has_state:
        in_specs += [pl.BlockSpec((sp, SUBLANES, 3 * gw), lambda b, n: (b, 0, 0)),
                     pl.BlockSpec((sp, n_heads, HEAD_DIM, HEAD_DIM), lambda b, n: (b, 0, 0, 0))]
        args += [st_qkv, s0]
    kern = functools.partial(_gdn_kernel, n_heads=n_heads, n_chunks=n_chunks, sp=sp, has_state=has_state)
    b_out, ns_qkv, s_fin = pl.pallas_call(
        kern,
        grid=(n_seq // sp, n_chunks),
        in_specs=in_specs,
        out_specs=[pl.BlockSpec((sp, CHUNK, gw), lambda b, n: (b, n, 0)),
                   pl.BlockSpec((sp, SUBLANES, 3 * gw), lambda b, n: (b, 0, 0)),
                   pl.BlockSpec((sp, n_heads, HEAD_DIM, HEAD_DIM), lambda b, n: (b, 0, 0, 0))],
        out_shape=[jax.ShapeDtypeStruct((n_seq, seq_len, gw), BF16),
                   jax.ShapeDtypeStruct((n_seq, SUBLANES, 3 * gw), F32),
                   jax.ShapeDtypeStruct((n_seq, n_heads, HEAD_DIM, HEAD_DIM), F32)],
        scratch_shapes=[pltpu.VMEM((sp, n_heads, HEAD_DIM, HEAD_DIM), F32),
                        pltpu.VMEM((sp, 3, SUBLANES, gw), F32)],
        compiler_params=_cparams("arbitrary", "arbitrary"),
        name="gdn",
    )(*args)
    return b_out.reshape(m, gw), ns_qkv, s_fin


def _outproj_kernel(a_ref, b_ref, wa_ref, wb_ref, x_ref, gpost_ref, gnext_ref, xo_ref, ho_ref):
    y = _dot(a_ref[...], wa_ref[...]) + _dot(b_ref[...], wb_ref[...])
    xn = x_ref[...] + _rms(y, gpost_ref[...])
    xo_ref[...] = xn
    ho_ref[...] = _rms(xn, gnext_ref[...]).astype(ho_ref.dtype)


def _outproj(a, b, w_out, x, g_post, g_next, *, tm):
    m, ka = a.shape
    kb = b.shape[1]
    d = w_out.shape[1]
    assert w_out.shape[0] == ka + kb and ka == kb and m % tm == 0
    row = lambda i: (i, 0)
    const = lambda i: (0, 0)
    return pl.pallas_call(
        _outproj_kernel,
        grid=(m // tm,),
        in_specs=[pl.BlockSpec((tm, ka), row), pl.BlockSpec((tm, kb), row),
                  pl.BlockSpec((ka, d), lambda i: (0, 0)), pl.BlockSpec((kb, d), lambda i: (1, 0)),
                  pl.BlockSpec((tm, d), row), pl.BlockSpec((1, d), const), pl.BlockSpec((1, d), const)],
        out_specs=[pl.BlockSpec((tm, d), row), pl.BlockSpec((tm, d), row)],
        out_shape=[jax.ShapeDtypeStruct((m, d), F32), jax.ShapeDtypeStruct((m, d), BF16)],
        compiler_params=_cparams("arbitrary"),
        name="outproj",
    )(a, b, w_out, w_out, x, g_post, g_next)


def _ffn_up_kernel(*refs, rows, spt, tps, has_state):
    if has_state:
        (h_ref, wg_ref, wu_ref, cwg_ref, cwu_ref, cbg_ref, cbu_ref, stg_ref, stu_ref,
         act_ref, nsg_ref, nsu_ref, carry) = refs
    else:
        (h_ref, wg_ref, wu_ref, cwg_ref, cwu_ref, cbg_ref, cbu_ref,
         act_ref, nsg_ref, nsu_ref, carry) = refs
        stg_ref = stu_ref = None
    tm = rows * spt
    tn = wg_ref.shape[1]
    i = pl.program_id(1)
    h = h_ref[...]
    assert rows & (rows - 1) == 0
    pos = lax.broadcasted_iota(jnp.int32, (tm, tn), 0) & (rows - 1)

    if tps > 1:
        @pl.when(i % tps == 0)
        def _():
            carry[...] = jnp.zeros(carry.shape, F32)

    outs = []
    for part, (w_ref, cw_ref, cb_ref, st_ref, ns_ref) in enumerate((
            (wg_ref, cwg_ref, cbg_ref, stg_ref, nsg_ref), (wu_ref, cwu_ref, cbu_ref, stu_ref, nsu_ref))):
        g = _dot(h, w_ref[...])
        prev1, prev2 = [], []
        for s in range(spt):
            if has_state:
                hrows = st_ref[s]
            elif tps > 1:
                hrows = carry[part]
            else:
                hrows = jnp.zeros((SUBLANES, tn), F32)
            prev1.append(jnp.broadcast_to(hrows[SUBLANES - 1:SUBLANES, :], (rows, tn)))
            prev2.append(jnp.broadcast_to(hrows[SUBLANES - 2:SUBLANES - 1, :], (rows, tn)))
        prev1 = prev1[0] if spt == 1 else jnp.concatenate(prev1, axis=0)
        prev2 = prev2[0] if spt == 1 else jnp.concatenate(prev2, axis=0)
        g1 = jnp.where(pos == 0, prev1, pltpu.roll(g, 1, axis=0))
        g2 = jnp.where(pos == 0, prev2, jnp.where(pos == 1, prev1, pltpu.roll(g, 2, axis=0)))
        cw = cw_ref[...]
        outs.append(cb_ref[...] + cw[2:3, :] * g + cw[1:2, :] * g1 + cw[0:1, :] * g2)
        for s in range(spt):
            tail = g[(s + 1) * rows - SUBLANES:(s + 1) * rows, :]
            ns_ref[s] = tail
        if tps > 1:
            carry[part] = g[tm - SUBLANES:tm, :]
    yg, yu = outs
    act_ref[...] = (yg * _sigmoid(yg) * yu).astype(act_ref.dtype)


def _ffn_up(h, w_up, cw, cb, state, *, n_seq, seq_len, tm, tn):
    m, d = h.shape
    ffn = w_up.shape[1] // 2
    nt = ffn // tn
    rows = min(tm, seq_len)
    spt = tm // rows
    tps = seq_len // rows
    assert m == n_seq * seq_len and m % tm == 0 and nt * tn == ffn and (spt == 1 or tps == 1)
    has_state = state is not None
    seq_idx = (lambda i: i // tps) if tps > 1 else (lambda i: i)
    in_specs = [
        pl.BlockSpec((tm, d), lambda j, i: (i, 0)),
        pl.BlockSpec((d, tn), lambda j, i: (0, j)), pl.BlockSpec((d, tn), lambda j, i: (0, j + nt)),
        pl.BlockSpec((FFN_TAPS, tn), lambda j, i: (0, j)), pl.BlockSpec((FFN_TAPS, tn), lambda j, i: (0, j + nt)),
        pl.BlockSpec((1, tn), lambda j, i: (0, j)), pl.BlockSpec((1, tn), lambda j, i: (0, j + nt)),
    ]
    args = [h, w_up, w_up, cw, cw, cb, cb]
    if has_state:
        in_specs += [pl.BlockSpec((spt, SUBLANES, tn), lambda j, i: (seq_idx(i), 0, j)),
                     pl.BlockSpec((spt, SUBLANES, tn), lambda j, i: (seq_idx(i), 0, j + nt))]
        args += [state, state]
    kern = functools.partial(_ffn_up_kernel, rows=rows, spt=spt, tps=tps, has_state=has_state)
    ns_spec = pl.BlockSpec((spt, SUBLANES, tn), lambda j, i: (seq_idx(i), 0, j))
    return pl.pallas_call(
        kern,
        grid=(nt, m // tm),
        in_specs=in_specs,
        out_specs=[pl.BlockSpec((tm, tn), lambda j, i: (i, j)), ns_spec, ns_spec],
        out_shape=[jax.ShapeDtypeStruct((m, ffn), BF16),
                   jax.ShapeDtypeStruct((n_seq, SUBLANES, ffn), F32),
                   jax.ShapeDtypeStruct((n_seq, SUBLANES, ffn), F32)],
        scratch_shapes=[pltpu.VMEM((2, SUBLANES, tn), F32)],
        compiler_params=_cparams("arbitrary", "arbitrary"),
        name="ffn_up",
    )(*args)


def _ffn_down_kernel(act_ref, w_ref, x_ref, gpost_ref, gnext_ref, xo_ref, ho_ref, acc):
    kk = pl.program_id(1)

    @pl.when(kk == 0)
    def _():
        acc[...] = jnp.zeros(acc.shape, F32)

    acc[...] += _dot(act_ref[...], w_ref[...])

    @pl.when(kk == pl.num_programs(1) - 1)
    def _():
        xn = x_ref[...] + _rms(acc[...], gpost_ref[...])
        xo_ref[...] = xn
        ho_ref[...] = _rms(xn, gnext_ref[...]).astype(ho_ref.dtype)


def _ffn_down(act, w_down, x, g_post, g_next, *, tm, tk):
    m, k = act.shape
    d = w_down.shape[1]
    assert m % tm == 0 and k % tk == 0
    row = lambda i, kk: (i, 0)
    const = lambda i, kk: (0, 0)
    return pl.pallas_call(
        _ffn_down_kernel,
        grid=(m // tm, k // tk),
        in_specs=[pl.BlockSpec((tm, tk), lambda i, kk: (i, kk)), pl.BlockSpec((tk, d), lambda i, kk: (kk, 0)),
                  pl.BlockSpec((tm, d), row), pl.BlockSpec((1, d), const), pl.BlockSpec((1, d), const)],
        out_specs=[pl.BlockSpec((tm, d), row), pl.BlockSpec((tm, d), row)],
        out_shape=[jax.ShapeDtypeStruct((m, d), F32), jax.ShapeDtypeStruct((m, d), BF16)],
        scratch_shapes=[pltpu.VMEM((tm, d), F32)],
        compiler_params=_cparams("arbitrary", "arbitrary"),
        name="ffn_down",
    )(act, w_down, x, g_post, g_next)


def _pad_front(state, rows):
    return jnp.pad(state, ((0, 0), (rows - state.shape[1], 0), (0, 0)))


def _layer(x, h, states, p, g_next, *, n_seq, seq_len, tm, gdn_sp=2):
    n_heads = p["a_log"].shape[-1]
    gw = n_heads * HEAD_DIM
    if states is None:
        st_conv = st_qkv = st_gdn = st_ffn = None
    else:
        st_conv = _pad_front(states[0], HIST)
        st_qkv = _pad_front(states[1], SUBLANES)
        st_gdn = states[2]
        st_ffn = _pad_front(states[3], SUBLANES)

    a_out, ba, ns_conv = _conformer(h, p["wv"], p["wg"], p["wba"], p["w_dw"], p["b_dw"], p["gn_g"], p["gn_b"],
                                    st_conv, n_seq=n_seq, seq_len=seq_len, tm=tm)
    qkvz = _matmul(h, p["wqkvz"], tm=tm, tn=512)
    b_out, ns_qkv, ns_gdn = _gdn(qkvz, ba, p["w_sc"], p["alog_row"], p["dtb_row"], p["onorm_g"], st_qkv, st_gdn,
                                 n_seq=n_seq, seq_len=seq_len, n_heads=n_heads, sp=gdn_sp)
    x, h = _outproj(a_out, b_out, p["w_out"], x, p["g_post_mix"], p["g_pre_ffn"], tm=tm)
    act, ns_fg, ns_fu = _ffn_up(h, p["w_up"], p["w_ffn_dw"], p["b_ffn_dw"], st_ffn,
                                n_seq=n_seq, seq_len=seq_len, tm=tm, tn=512)
    x, h = _ffn_down(act, p["w_down"], x, p["g_post_ffn"], g_next, tm=tm, tk=1408)
    new_states = (ns_conv[:, HIST - (CONV_TAPS - 1):, :],
                  ns_qkv[:, SUBLANES - (SHORT_TAPS - 1):, :],
                  ns_gdn,
                  jnp.concatenate([ns_fg, ns_fu], axis=-1)[:, SUBLANES - (FFN_TAPS - 1):, :])
    return x, h, new_states


def _layer_params(l, g_pre_mix, w_in, w_dw, b_dw, gn_g, gn_b, w_sc, a_log, dt_bias, onorm_g,
                  w_out, g_post_mix, g_pre_ffn, w_up, w_ffn_dw, b_ffn_dw, w_down, g_post_ffn):
    ch = w_dw.shape[-1]
    n_heads = a_log.shape[-1]
    gw = n_heads * HEAD_DIM
    c1 = 2 * ch
    c3 = c1 + 4 * gw
    wl = w_in[l]
    wba = jnp.pad(wl[:, c3:c3 + 2 * n_heads], ((0, 0), (0, LANES - 2 * n_heads)))
    row = lambda v: v[l][None, :]
    gate_cols = lambda v: jnp.pad(v[l], (n_heads, LANES - 2 * n_heads))[None, :]
    return dict(
        wv=wl[:, :ch].astype(BF16), wg=wl[:, ch:c1].astype(BF16), wqkvz=wl[:, c1:c3].astype(BF16),
        wba=wba.astype(BF16),
        w_dw=w_dw[l], b_dw=row(b_dw), gn_g=row(gn_g), gn_b=row(gn_b),
        w_sc=w_sc[l], alog_row=gate_cols(a_log), dtb_row=gate_cols(dt_bias), onorm_g=row(onorm_g),
        a_log=a_log[l],
        w_out=w_out[l].astype(BF16), g_post_mix=row(g_post_mix), g_pre_ffn=row(g_pre_ffn),
        w_up=w_up[l].astype(BF16), w_ffn_dw=w_ffn_dw[l], b_ffn_dw=row(b_ffn_dw),
        w_down=w_down[l].astype(BF16), g_post_ffn=row(g_post_ffn),
    )


def kernel(x_prompt, x_sample, state_conv, state_qkv_conv, state_gdn, state_ffn_conv, g_pre_mix, w_in, w_dw, b_dw, gn_g, gn_b, w_sc, a_log, dt_bias, onorm_g, w_out, g_post_mix, g_pre_ffn, w_up, w_ffn_dw, b_ffn_dw, w_down, g_post_ffn):
    depth = w_in.shape[0]
    bp, tp, d = x_prompt.shape
    bs, ts, _ = x_sample.shape
    tm = 512
    xp = x_prompt.reshape(bp * tp, d)
    xs = x_sample.reshape(bs * ts, d)
    hp = _rmsnorm(xp, g_pre_mix[0][None, :], tm)
    hs = _rmsnorm(xs, g_pre_mix[0][None, :], tm)
    outs_p, outs_s = [], []
    for l in range(depth):
        p = _layer_params(l, g_pre_mix, w_in, w_dw, b_dw, gn_g, gn_b, w_sc, a_log, dt_bias, onorm_g,
                          w_out, g_post_mix, g_pre_ffn, w_up, w_ffn_dw, b_ffn_dw, w_down, g_post_ffn)
        g_next = g_pre_mix[(l + 1) % depth][None, :]
        xp, hp, st_p = _layer(xp, hp, None, p, g_next, n_seq=bp, seq_len=tp, tm=tm)
        xs, hs, st_s = _layer(xs, hs, (state_conv[l], state_qkv_conv[l], state_gdn[l], state_ffn_conv[l]),
                              p, g_next, n_seq=bs, seq_len=ts, tm=tm)
        outs_p.append(st_p)
        outs_s.append(st_s)
    stack = lambda outs, j: jnp.stack([o[j] for o in outs])
    return (xp.reshape(bp, tp, d), xs.reshape(bs, ts, d),
            stack(outs_p, 0), stack(outs_p, 1), stack(outs_p, 2), stack(outs_p, 3),
            stack(outs_s, 0), stack(outs_s, 1), stack(outs_s, 2), stack(outs_s, 3))
```

```python
import functools

import jax
import jax.numpy as jnp
from jax import lax
from jax.experimental import pallas as pl
from jax.experimental.pallas import tpu as pltpu

F32 = jnp.float32
BF16 = jnp.bfloat16

EPS = 1e-6
HEAD_DIM = 128
CONV_GROUP = 128
CONV_TAPS = 31
SHORT_TAPS = 4
FFN_TAPS = 3
CHUNK = 64
HIST = 32
SUBLANES = 8
LANES = 128
VMEM_LIMIT_BYTES = 56 * 1024 * 1024


def _cparams(*semantics):
    return pltpu.CompilerParams(dimension_semantics=semantics, vmem_limit_bytes=VMEM_LIMIT_BYTES)


def _sigmoid(x):
    return 1.0 / (1.0 + jnp.exp(-x))


def _softplus(x):
    return jnp.maximum(x, 0.0) + jnp.log(1.0 + jnp.exp(-jnp.abs(x)))


def _dot(a, b):
    return jnp.dot(a, b, preferred_element_type=F32)


def _dot_nt(a, b):
    return lax.dot_general(a, b, (((1,), (1,)), ((), ())), preferred_element_type=F32)


def _dot_tn(a, b):
    return lax.dot_general(a, b, (((0,), (0,)), ((), ())), preferred_element_type=F32)


def _rms(x, g):
    return x * lax.rsqrt(jnp.mean(x * x, axis=-1, keepdims=True) + EPS) * g


def _rmsnorm_kernel(x_ref, g_ref, h_ref):
    h_ref[...] = _rms(x_ref[...], g_ref[...]).astype(h_ref.dtype)


def _rmsnorm(x, g, tm):
    m, d = x.shape
    return pl.pallas_call(
        _rmsnorm_kernel,
        grid=(m // tm,),
        in_specs=[pl.BlockSpec((tm, d), lambda i: (i, 0)), pl.BlockSpec((1, d), lambda i: (0, 0))],
        out_specs=pl.BlockSpec((tm, d), lambda i: (i, 0)),
        out_shape=jax.ShapeDtypeStruct((m, d), BF16),
        compiler_params=_cparams("arbitrary"),
        name="rmsnorm",
    )(x, g)


def _conformer_kernel(*refs, rows, spt, tps, has_state, row_chunk):
    if has_state:
        (h_ref, wv_ref, wg_ref, wba_ref, wdw_ref, bdw_ref, gng_ref, gnb_ref, st_ref,
         a_ref, ba_ref, ns_ref, u_buf) = refs
    else:
        (h_ref, wv_ref, wg_ref, wba_ref, wdw_ref, bdw_ref, gng_ref, gnb_ref,
         a_ref, ba_ref, ns_ref, u_buf) = refs
        st_ref = None
    ch = wv_ref.shape[1]
    n_groups = ch // CONV_GROUP
    i = pl.program_id(0)

    h = h_ref[...]
    val = _dot(h, wv_ref[...])
    gate = _dot(h, wg_ref[...])
    ba_ref[...] = _dot(h, wba_ref[...])
    u = val * _sigmoid(gate)

    def init_history():
        for s in range(spt):
            u_buf[s, 0:HIST, :] = st_ref[s] if has_state else jnp.zeros((HIST, ch), F32)

    if tps > 1:
        pl.when(i % tps == 0)(init_history)
    else:
        init_history()
    for s in range(spt):
        u_buf[s, HIST:HIST + rows, :] = u[s * rows:(s + 1) * rows, :]

    n_rc = rows // row_chunk
    win = row_chunk + HIST
    lead = HIST - (CONV_TAPS - 1)

    def chunk_body(idx, carry):
        s = idx // n_rc
        t0 = pl.multiple_of((idx % n_rc) * row_chunk, row_chunk)
        for c in range(n_groups):
            lanes = slice(c * CONV_GROUP, (c + 1) * CONV_GROUP)
            x = u_buf[s, pl.ds(t0, win), lanes]
            acc = jnp.broadcast_to(bdw_ref[:, lanes], (row_chunk, CONV_GROUP))
            for shift in range(SUBLANES):
                xs = x if shift == 0 else pltpu.roll(x, win - shift, axis=0)
                for k in range(CONV_TAPS):
                    if (lead + k) % SUBLANES != shift:
                        continue
                    off = lead + k - shift
                    acc = acc + wdw_ref[k:k + 1, lanes] * xs[off:off + row_chunk, :]
            mu = jnp.mean(acc, axis=-1, keepdims=True)
            d = acc - mu
            var = jnp.mean(d * d, axis=-1, keepdims=True)
            un = d * lax.rsqrt(var + EPS) * gng_ref[:, lanes] + gnb_ref[:, lanes]
            r0 = pl.multiple_of(s * rows + t0, row_chunk)
            a_ref[pl.ds(r0, row_chunk), lanes] = (un * _sigmoid(un)).astype(a_ref.dtype)
        return carry

    lax.fori_loop(0, spt * n_rc, chunk_body, 0)

    for s in range(spt):
        tail = u_buf[s, rows:rows + HIST, :]
        ns_ref[s] = tail
        if tps > 1:
            u_buf[s, 0:HIST, :] = tail


def _conformer(h, wv, wg, wba, w_dw, b_dw, gn_g, gn_b, state, *, n_seq, seq_len, tm):
    m, d = h.shape
    ch = wv.shape[1]
    rows = min(tm, seq_len)
    spt = tm // rows
    tps = seq_len // rows
    assert m == n_seq * seq_len and m % tm == 0 and rows * spt == tm and rows * tps == seq_len
    assert spt == 1 or tps == 1
    row_chunk = min(64, rows)
    seq_block = (lambda i: (i // tps, 0, 0)) if tps > 1 else (lambda i: (i, 0, 0))
    const = lambda i: (0, 0)
    in_specs = [
        pl.BlockSpec((tm, d), lambda i: (i, 0)),
        pl.BlockSpec((d, ch), const), pl.BlockSpec((d, ch), const), pl.BlockSpec((d, LANES), const),
        pl.BlockSpec((CONV_TAPS, ch), const), pl.BlockSpec((1, ch), const),
        pl.BlockSpec((1, ch), const), pl.BlockSpec((1, ch), const),
    ]
    args = [h, wv, wg, wba, w_dw, b_dw, gn_g, gn_b]
    if state is not None:
        in_specs.append(pl.BlockSpec((spt, HIST, ch), seq_block))
        args.append(state)
    kern = functools.partial(_conformer_kernel, rows=rows, spt=spt, tps=tps,
                             has_state=state is not None, row_chunk=row_chunk)
    return pl.pallas_call(
        kern,
        grid=(m // tm,),
        in_specs=in_specs,
        out_specs=[pl.BlockSpec((tm, ch), lambda i: (i, 0)),
                   pl.BlockSpec((tm, LANES), lambda i: (i, 0)),
                   pl.BlockSpec((spt, HIST, ch), seq_block)],
        out_shape=[jax.ShapeDtypeStruct((m, ch), BF16),
                   jax.ShapeDtypeStruct((m, LANES), F32),
                   jax.ShapeDtypeStruct((n_seq, HIST, ch), F32)],
        scratch_shapes=[pltpu.VMEM((spt, HIST + rows, ch), F32)],
        compiler_params=_cparams("arbitrary"),
        name="conformer",
    )(*args)


def _matmul_kernel(h_ref, w_ref, o_ref):
    o_ref[...] = _dot(h_ref[...], w_ref[...]).astype(o_ref.dtype)


def _matmul(h, w, *, tm, tn):
    m, k = h.shape
    n = w.shape[1]
    assert m % tm == 0 and n % tn == 0
    return pl.pallas_call(
        _matmul_kernel,
        grid=(n // tn, m // tm),
        in_specs=[pl.BlockSpec((tm, k), lambda j, i: (i, 0)), pl.BlockSpec((k, tn), lambda j, i: (0, j))],
        out_specs=pl.BlockSpec((tm, tn), lambda j, i: (i, j)),
        out_shape=jax.ShapeDtypeStruct((m, n), F32),
        compiler_params=_cparams("arbitrary", "arbitrary"),
        name="qkvz_proj",
    )(h, w)


def _gdn_kernel(*refs, n_heads, n_chunks, sp, has_state):
    if has_state:
        (q_ref, k_ref, v_ref, z_ref, ba_ref, wsc_ref, alog_ref, dtb_ref, on_ref, st_ref, s0_ref,
         o_ref, nst_ref, sfin_ref, s_buf, hist) = refs
    else:
        (q_ref, k_ref, v_ref, z_ref, ba_ref, wsc_ref, alog_ref, dtb_ref, on_ref,
         o_ref, nst_ref, sfin_ref, s_buf, hist) = refs
        st_ref = s0_ref = None
    L = CHUNK
    gw = n_heads * HEAD_DIM
    n = pl.program_id(1)

    @pl.when(n == 0)
    def _():
        if has_state:
            s_buf[...] = s0_ref[...]
            for p in range(3):
                hist[:, p] = st_ref[:, :, p * gw:(p + 1) * gw]
        else:
            s_buf[...] = jnp.zeros(s_buf.shape, F32)
            hist[...] = jnp.zeros(hist.shape, F32)

    row = lax.broadcasted_iota(jnp.int32, (L, LANES), 0)
    lane = lax.broadcasted_iota(jnp.int32, (L, LANES), 1)
    incl = (lane <= row)
    strict = (lane < row)
    eye_hi = (lane == row + L).astype(F32)
    right = lane >= L
    zeros_k = jnp.zeros((L, HEAD_DIM), BF16)

    def sequence_prologue(s):
        conv = []
        for p, x_ref in enumerate((q_ref, k_ref, v_ref)):
            x = x_ref[s]
            xe = jnp.concatenate([hist[s, p], x], axis=0)
            w = wsc_ref[:, p * gw:(p + 1) * gw]
            y = w[SHORT_TAPS - 1:SHORT_TAPS, :] * x
            for j in range(1, SHORT_TAPS):
                xj = pltpu.roll(xe, j, axis=0)[SUBLANES:SUBLANES + L, :]
                y = y + w[SHORT_TAPS - 1 - j:SHORT_TAPS - j, :] * xj
            conv.append(y * _sigmoid(y))
            tail = x[L - SUBLANES:L, :]
            hist[s, p] = tail
            nst_ref[s, :, p * gw:(p + 1) * gw] = tail
        ba = ba_ref[s]
        beta_all = _sigmoid(ba)
        gcum = -jnp.exp(alog_ref[...]) * _softplus(ba + dtb_ref[...])
        d = 1
        while d < L:
            gcum = gcum + jnp.where(row >= d, pltpu.roll(gcum, d, axis=0), 0.0)
            d *= 2
        g_last = gcum[L - 1:L, :]
        gates = dict(beta=beta_all, gcum=gcum, eg=jnp.exp(gcum), ekd=jnp.exp(g_last - gcum), gl=jnp.exp(g_last),
                     gcum_t=jnp.concatenate([gcum, jnp.zeros((LANES - L, LANES), F32)], axis=0).T)
        return conv, gates

    def head_chain(s, hd, conv, gates):
        qc, kc, vc = conv
        lanes = slice(hd * HEAD_DIM, (hd + 1) * HEAD_DIM)
        cb, ca = hd, n_heads + hd
        beta = gates["beta"][:, cb:cb + 1]
        gcol = gates["gcum"][:, ca:ca + 1]
        grow = gates["gcum_t"][ca:ca + 1, :]
        eg = gates["eg"][:, ca:ca + 1]
        ekd = gates["ekd"][:, ca:ca + 1]
        gl = gates["gl"][:, ca:ca + 1]

        q = qc[:, lanes]
        k = kc[:, lanes]
        v = vc[:, lanes]
        qn = q * lax.rsqrt(jnp.sum(q * q, axis=-1, keepdims=True) + EPS) * (HEAD_DIM ** -0.5)
        kn = k * lax.rsqrt(jnp.sum(k * k, axis=-1, keepdims=True) + EPS)
        kb = kn * beta
        a_raw = _dot_nt(jnp.concatenate([kb, qn], axis=0).astype(BF16),
                        jnp.concatenate([kn.astype(BF16), zeros_k], axis=0))
        yield
        decay = jnp.exp(jnp.where(incl, gcol - grow, -jnp.inf))
        qk = a_raw[L:2 * L, :] * decay
        x = jnp.where(strict, -(a_raw[0:L, :] * decay), 0.0) + eye_hi
        for _ in range(5):
            x = _dot(x[:, 0:L].astype(BF16), x.astype(BF16)) + jnp.where(right, x, 0.0)
            yield
        xr = pltpu.roll(x, L, axis=1)
        t = _dot(x[:, 0:L].astype(BF16), xr.astype(BF16)) + xr
        yield
        rhs = jnp.concatenate([v * beta, kb * eg], axis=1).astype(BF16)
        uw = _dot(t[:, 0:L].astype(BF16), rhs)
        yield
        u = uw[:, 0:HEAD_DIM]
        w = uw[:, HEAD_DIM:2 * HEAD_DIM]
        st = s_buf[s, hd]
        m1 = _dot(jnp.concatenate([w, qn * eg], axis=0).astype(BF16), st.astype(BF16))
        yield
        vn_bf = (u - m1[0:L, :]).astype(BF16)
        o = m1[L:2 * L, :] + _dot(qk[:, 0:L].astype(BF16), vn_bf)
        yield
        s_buf[s, hd] = st * gl + _dot_tn((kn * ekd).astype(BF16), vn_bf)
        o = o * lax.rsqrt(jnp.mean(o * o, axis=-1, keepdims=True) + EPS) * on_ref[...]
        zz = z_ref[s, :, lanes]
        o_ref[s, :, lanes] = (o * (zz * _sigmoid(zz))).astype(o_ref.dtype)

    chains = []
    for s in range(sp):
        conv, gates = sequence_prologue(s)
        chains += [head_chain(s, hd, conv, gates) for hd in range(n_heads)]
    while chains:
        alive = []
        for c in chains:
            try:
                next(c)
                alive.append(c)
            except StopIteration:
                pass
        chains = alive

    @pl.when(n == n_chunks - 1)
    def _():
        sfin_ref[...] = s_buf[...]


def _gdn(qkvz, ba, w_sc, alog_row, dtb_row, onorm_g, st_qkv, s0, *, n_seq, seq_len, n_heads, sp):
    m = qkvz.shape[0]
    gw = n_heads * HEAD_DIM
    n_chunks = seq_len // CHUNK
    assert m == n_seq * seq_len and n_chunks * CHUNK == seq_len and n_seq % sp == 0
    qkvz3 = qkvz.reshape(n_seq, seq_len, 4 * gw)
    ba3 = ba.reshape(n_seq, seq_len, LANES)
    const = lambda b, n: (0, 0)
    in_specs = [pl.BlockSpec((sp, CHUNK, gw), lambda b, n, p=p: (b, n, p)) for p in range(4)]
    in_specs += [
        pl.BlockSpec((sp, CHUNK, LANES), lambda b, n: (b, n, 0)),
        pl.BlockSpec((SHORT_TAPS, 3 * gw), const),
        pl.BlockSpec((1, LANES), const), pl.BlockSpec((1, LANES), const), pl.BlockSpec((1, HEAD_DIM), const),
    ]
    args = [qkvz3, qkvz3, qkvz3, qkvz3, ba3, w_sc, alog_row, dtb_row, onorm_g]
    has_state = st_qkv is not None
    if has_state:
        in_specs += [pl.BlockSpec((sp, SUBLANES, 3 * gw), lambda b, n: (b, 0, 0)),
                     pl.BlockSpec((sp, n_heads, HEAD_DIM, HEAD_DIM), lambda b, n: (b, 0, 0, 0))]
        args += [st_qkv, s0]
    kern = functools.partial(_gdn_kernel, n_heads=n_heads, n_chunks=n_chunks, sp=sp, has_state=has_state)
    b_out, ns_qkv, s_fin = pl.pallas_call(
        kern,
        grid=(n_seq // sp, n_chunks),
        in_specs=in_specs,
        out_specs=[pl.BlockSpec((sp, CHUNK, gw), lambda b, n: (b, n, 0)),
                   pl.BlockSpec((sp, SUBLANES, 3 * gw), lambda b, n: (b, 0, 0)),
                   pl.BlockSpec((sp, n_heads, HEAD_DIM, HEAD_DIM), lambda b, n: (b, 0, 0, 0))],
        out_shape=[jax.ShapeDtypeStruct((n_seq, seq_len, gw), BF16),
                   jax.ShapeDtypeStruct((n_seq, SUBLANES, 3 * gw), F32),
                   jax.ShapeDtypeStruct((n_seq, n_heads, HEAD_DIM, HEAD_DIM), F32)],
        scratch_shapes=[pltpu.VMEM((sp, n_heads, HEAD_DIM, HEAD_DIM), F32),
                        pltpu.VMEM((sp, 3, SUBLANES, gw), F32)],
        compiler_params=_cparams("arbitrary", "arbitrary"),
        name="gdn",
    )(*args)
    return b_out.reshape(m, gw), ns_qkv, s_fin


def _outproj_kernel(a_ref, b_ref, wa_ref, wb_ref, x_ref, gpost_ref, gnext_ref, xo_ref, ho_ref):
    y = _dot(a_ref[...], wa_ref[...]) + _dot(b_ref[...], wb_ref[...])
    xn = x_ref[...] + _rms(y, gpost_ref[...])
    xo_ref[...] = xn
    ho_ref[...] = _rms(xn, gnext_ref[...]).astype(ho_ref.dtype)


def _outproj(a, b, w_out, x, g_post, g_next, *, tm):
    m, ka = a.shape
    kb = b.shape[1]
    d = w_out.shape[1]
    assert w_out.shape[0] == ka + kb and ka == kb and m % tm == 0
    row = lambda i: (i, 0)
    const = lambda i: (0, 0)
    return pl.pallas_call(
        _outproj_kernel,
        grid=(m // tm,),
        in_specs=[pl.BlockSpec((tm, ka), row), pl.BlockSpec((tm, kb), row),
                  pl.BlockSpec((ka, d), lambda i: (0, 0)), pl.BlockSpec((kb, d), lambda i: (1, 0)),
                  pl.BlockSpec((tm, d), row), pl.BlockSpec((1, d), const), pl.BlockSpec((1, d), const)],
        out_specs=[pl.BlockSpec((tm, d), row), pl.BlockSpec((tm, d), row)],
        out_shape=[jax.ShapeDtypeStruct((m, d), F32), jax.ShapeDtypeStruct((m, d), BF16)],
        compiler_params=_cparams("arbitrary"),
        name="outproj",
    )(a, b, w_out, w_out, x, g_post, g_next)


def _ffn_up_kernel(*refs, rows, spt, tps, has_state):
    if has_state:
        (h_ref, wg_ref, wu_ref, cwg_ref, cwu_ref, cbg_ref, cbu_ref, stg_ref, stu_ref,
         act_ref, nsg_ref, nsu_ref, carry) = refs
    else:
        (h_ref, wg_ref, wu_ref, cwg_ref, cwu_ref, cbg_ref, cbu_ref,
         act_ref, nsg_ref, nsu_ref, carry) = refs
        stg_ref = stu_ref = None
    tm = rows * spt
    tn = wg_ref.shape[1]
    i = pl.program_id(1)
    h = h_ref[...]
    assert rows & (rows - 1) == 0
    pos = lax.broadcasted_iota(jnp.int32, (tm, tn), 0) & (rows - 1)

    if tps > 1:
        @pl.when(i % tps == 0)
        def _():
            carry[...] = jnp.zeros(carry.shape, F32)

    outs = []
    for part, (w_ref, cw_ref, cb_ref, st_ref, ns_ref) in enumerate((
            (wg_ref, cwg_ref, cbg_ref, stg_ref, nsg_ref), (wu_ref, cwu_ref, cbu_ref, stu_ref, nsu_ref))):
        g = _dot(h, w_ref[...])
        prev1, prev2 = [], []
        for s in range(spt):
            if has_state:
                hrows = st_ref[s]
            elif tps > 1:
                hrows = carry[part]
            else:
                hrows = jnp.zeros((SUBLANES, tn), F32)
            prev1.append(jnp.broadcast_to(hrows[SUBLANES - 1:SUBLANES, :], (rows, tn)))
            prev2.append(jnp.broadcast_to(hrows[SUBLANES - 2:SUBLANES - 1, :], (rows, tn)))
        prev1 = prev1[0] if spt == 1 else jnp.concatenate(prev1, axis=0)
        prev2 = prev2[0] if spt == 1 else jnp.concatenate(prev2, axis=0)
        g1 = jnp.where(pos == 0, prev1, pltpu.roll(g, 1, axis=0))
        g2 = jnp.where(pos == 0, prev2, jnp.where(pos == 1, prev1, pltpu.roll(g, 2, axis=0)))
        cw = cw_ref[...]
        outs.append(cb_ref[...] + cw[2:3, :] * g + cw[1:2, :] * g1 + cw[0:1, :] * g2)
        for s in range(spt):
            tail = g[(s + 1) * rows - SUBLANES:(s + 1) * rows, :]
            ns_ref[s] = tail
        if tps > 1:
            carry[part] = g[tm - SUBLANES:tm, :]
    yg, yu = outs
    act_ref[...] = (yg * _sigmoid(yg) * yu).astype(act_ref.dtype)


def _ffn_up(h, w_up, cw, cb, state, *, n_seq, seq_len, tm, tn):
    m, d = h.shape
    ffn = w_up.shape[1] // 2
    nt = ffn // tn
    rows = min(tm, seq_len)
    spt = tm // rows
    tps = seq_len // rows
    assert m == n_seq * seq_len and m % tm == 0 and nt * tn == ffn and (spt == 1 or tps == 1)
    has_state = state is not None
    seq_idx = (lambda i: i // tps) if tps > 1 else (lambda i: i)
    in_specs = [
        pl.BlockSpec((tm, d), lambda j, i: (i, 0)),
        pl.BlockSpec((d, tn), lambda j, i: (0, j)), pl.BlockSpec((d, tn), lambda j, i: (0, j + nt)),
        pl.BlockSpec((FFN_TAPS, tn), lambda j, i: (0, j)), pl.BlockSpec((FFN_TAPS, tn), lambda j, i: (0, j + nt)),
        pl.BlockSpec((1, tn), lambda j, i: (0, j)), pl.BlockSpec((1, tn), lambda j, i: (0, j + nt)),
    ]
    args = [h, w_up, w_up, cw, cw, cb, cb]
    if has_state:
        in_specs += [pl.BlockSpec((spt, SUBLANES, tn), lambda j, i: (seq_idx(i), 0, j)),
                     pl.BlockSpec((spt, SUBLANES, tn), lambda j, i: (seq_idx(i), 0, j + nt))]
        args += [state, state]
    kern = functools.partial(_ffn_up_kernel, rows=rows, spt=spt, tps=tps, has_state=has_state)
    ns_spec = pl.BlockSpec((spt, SUBLANES, tn), lambda j, i: (seq_idx(i), 0, j))
    return pl.pallas_call(
        kern,
        grid=(nt, m // tm),
        in_specs=in_specs,
        out_specs=[pl.BlockSpec((tm, tn), lambda j, i: (i, j)), ns_spec, ns_spec],
        out_shape=[jax.ShapeDtypeStruct((m, ffn), BF16),
                   jax.ShapeDtypeStruct((n_seq, SUBLANES, ffn), F32),
                   jax.ShapeDtypeStruct((n_seq, SUBLANES, ffn), F32)],
        scratch_shapes=[pltpu.VMEM((2, SUBLANES, tn), F32)],
        compiler_params=_cparams("arbitrary", "arbitrary"),
        name="ffn_up",
    )(*args)


def _ffn_down_kernel(act_ref, w_ref, x_ref, gpost_ref, gnext_ref, xo_ref, ho_ref, acc):
    kk = pl.program_id(1)

    @pl.when(kk == 0)
    def _():
        acc[...] = jnp.zeros(acc.shape, F32)

    acc[...] += _dot(act_ref[...], w_ref[...])

    @pl.when(kk == pl.num_programs(1) - 1)
    def _():
        xn = x_ref[...] + _rms(acc[...], gpost_ref[...])
        xo_ref[...] = xn
        ho_ref[...] = _rms(xn, gnext_ref[...]).astype(ho_ref.dtype)


def _ffn_down(act, w_down, x, g_post, g_next, *, tm, tk):
    m, k = act.shape
    d = w_down.shape[1]
    assert m % tm == 0 and k % tk == 0
    row = lambda i, kk: (i, 0)
    const = lambda i, kk: (0, 0)
    return pl.pallas_call(
        _ffn_down_kernel,
        grid=(m // tm, k // tk),
        in_specs=[pl.BlockSpec((tm, tk), lambda i, kk: (i, kk)), pl.BlockSpec((tk, d), lambda i, kk: (kk, 0)),
                  pl.BlockSpec((tm, d), row), pl.BlockSpec((1, d), const), pl.BlockSpec((1, d), const)],
        out_specs=[pl.BlockSpec((tm, d), row), pl.BlockSpec((tm, d), row)],
        out_shape=[jax.ShapeDtypeStruct((m, d), F32), jax.ShapeDtypeStruct((m, d), BF16)],
        scratch_shapes=[pltpu.VMEM((tm, d), F32)],
        compiler_params=_cparams("arbitrary", "arbitrary"),
        name="ffn_down",
    )(act, w_down, x, g_post, g_next)


def _pad_front(state, rows):
    return jnp.pad(state, ((0, 0), (rows - state.shape[1], 0), (0, 0)))


def _layer(x, h, states, p, g_next, *, n_seq, seq_len, tm, gdn_sp=2):
    n_heads = p["a_log"].shape[-1]
    gw = n_heads * HEAD_DIM
    if states is None:
        st_conv = st_qkv = st_gdn = st_ffn = None
    else:
        st_conv = _pad_front(states[0], HIST)
        st_qkv = _pad_front(states[1], SUBLANES)
        st_gdn = states[2]
        st_ffn = _pad_front(states[3], SUBLANES)

    a_out, ba, ns_conv = _conformer(h, p["wv"], p["wg"], p["wba"], p["w_dw"], p["b_dw"], p["gn_g"], p["gn_b"],
                                    st_conv, n_seq=n_seq, seq_len=seq_len, tm=tm)
    qkvz = _matmul(h, p["wqkvz"], tm=2 * tm, tn=1024)
    b_out, ns_qkv, ns_gdn = _gdn(qkvz, ba, p["w_sc"], p["alog_row"], p["dtb_row"], p["onorm_g"], st_qkv, st_gdn,
                                 n_seq=n_seq, seq_len=seq_len, n_heads=n_heads, sp=gdn_sp)
    x, h = _outproj(a_out, b_out, p["w_out"], x, p["g_post_mix"], p["g_pre_ffn"], tm=tm)
    act, ns_fg, ns_fu = _ffn_up(h, p["w_up"], p["w_ffn_dw"], p["b_ffn_dw"], st_ffn,
                                n_seq=n_seq, seq_len=seq_len, tm=tm, tn=512)
    x, h = _ffn_down(act, p["w_down"], x, p["g_post_ffn"], g_next, tm=tm, tk=1408)
    new_states = (ns_conv[:, HIST - (CONV_TAPS - 1):, :],
                  ns_qkv[:, SUBLANES - (SHORT_TAPS - 1):, :],
                  ns_gdn,
                  jnp.concatenate([ns_fg, ns_fu], axis=-1)[:, SUBLANES - (FFN_TAPS - 1):, :])
    return x, h, new_states


def _layer_params(l, g_pre_mix, w_in, w_dw, b_dw, gn_g, gn_b, w_sc, a_log, dt_bias, onorm_g,
                  w_out, g_post_mix, g_pre_ffn, w_up, w_ffn_dw, b_ffn_dw, w_down, g_post_ffn):
    ch = w_dw.shape[-1]
    n_heads = a_log.shape[-1]
    gw = n_heads * HEAD_DIM
    c1 = 2 * ch
    c3 = c1 + 4 * gw
    wl = w_in[l]
    wba = jnp.pad(wl[:, c3:c3 + 2 * n_heads], ((0, 0), (0, LANES - 2 * n_heads)))
    row = lambda v: v[l][None, :]
    gate_cols = lambda v: jnp.pad(v[l], (n_heads, LANES - 2 * n_heads))[None, :]
    return dict(
        wv=wl[:, :ch].astype(BF16), wg=wl[:, ch:c1].astype(BF16), wqkvz=wl[:, c1:c3].astype(BF16),
        wba=wba.astype(BF16),
        w_dw=w_dw[l], b_dw=row(b_dw), gn_g=row(gn_g), gn_b=row(gn_b),
        w_sc=w_sc[l], alog_row=gate_cols(a_log), dtb_row=gate_cols(dt_bias), onorm_g=row(onorm_g),
        a_log=a_log[l],
        w_out=w_out[l].astype(BF16), g_post_mix=row(g_post_mix), g_pre_ffn=row(g_pre_ffn),
        w_up=w_up[l].astype(BF16), w_ffn_dw=w_ffn_dw[l], b_ffn_dw=row(b_ffn_dw),
        w_down=w_down[l].astype(BF16), g_post_ffn=row(g_post_ffn),
    )


def kernel(x_prompt, x_sample, state_conv, state_qkv_conv, state_gdn, state_ffn_conv, g_pre_mix, w_in, w_dw, b_dw, gn_g, gn_b, w_sc, a_log, dt_bias, onorm_g, w_out, g_post_mix, g_pre_ffn, w_up, w_ffn_dw, b_ffn_dw, w_down, g_post_ffn):
    depth = w_in.shape[0]
    bp, tp, d = x_prompt.shape
    bs, ts, _ = x_sample.shape
    tm = 512
    xp = x_prompt.reshape(bp * tp, d)
    xs = x_sample.reshape(bs * ts, d)
    hp = _rmsnorm(xp, g_pre_mix[0][None, :], tm)
    hs = _rmsnorm(xs, g_pre_mix[0][None, :], tm)
    outs_p, outs_s = [], []
    for l in range(depth):
        p = _layer_params(l, g_pre_mix, w_in, w_dw, b_dw, gn_g, gn_b, w_sc, a_log, dt_bias, onorm_g,
                          w_out, g_post_mix, g_pre_ffn, w_up, w_ffn_dw, b_ffn_dw, w_down, g_post_ffn)
        g_next = g_pre_mix[(l + 1) % depth][None, :]
        xp, hp, st_p = _layer(xp, hp, None, p, g_next, n_seq=bp, seq_len=tp, tm=tm)
        xs, hs, st_s = _layer(xs, hs, (state_conv[l], state_qkv_conv[l], state_gdn[l], state_ffn_conv[l]),
                              p, g_next, n_seq=bs, seq_len=ts, tm=tm)
        outs_p.append(st_p)
        outs_s.append(st_s)
    stack = lambda outs, j: jnp.stack([o[j] for o in outs])
    return (xp.reshape(bp, tp, d), xs.reshape(bs, ts, d),
            stack(outs_p, 0), stack(outs_p, 1), stack(outs_p, 2), stack(outs_p, 3),
            stack(outs_s, 0), stack(outs_s, 1), stack(outs_s, 2), stack(outs_s, 3))
```

```python
import functools

import jax
import jax.numpy as jnp
from jax import lax
from jax.experimental import pallas as pl
from jax.experimental.pallas import tpu as pltpu

F32 = jnp.float32
BF16 = jnp.bfloat16

EPS = 1e-6
HEAD_DIM = 128
CONV_GROUP = 128
CONV_TAPS = 31
SHORT_TAPS = 4
FFN_TAPS = 3
CHUNK = 64
HIST = 32
SUBLANES = 8
LANES = 128
VMEM_LIMIT_BYTES = 56 * 1024 * 1024


def _cparams(*semantics):
    return pltpu.CompilerParams(dimension_semantics=semantics, vmem_limit_bytes=VMEM_LIMIT_BYTES)


def _sigmoid(x):
    return 1.0 / (1.0 + jnp.exp(-x))


def _softplus(x):
    return jnp.maximum(x, 0.0) + jnp.log(1.0 + jnp.exp(-jnp.abs(x)))


def _dot(a, b):
    return jnp.dot(a, b, preferred_element_type=F32)


def _dot_nt(a, b):
    return lax.dot_general(a, b, (((1,), (1,)), ((), ())), preferred_element_type=F32)


def _dot_tn(a, b):
    return lax.dot_general(a, b, (((0,), (0,)), ((), ())), preferred_element_type=F32)


def _rms(x, g):
    return x * lax.rsqrt(jnp.mean(x * x, axis=-1, keepdims=True) + EPS) * g


def _rmsnorm_kernel(x_ref, g_ref, h_ref):
    h_ref[...] = _rms(x_ref[...], g_ref[...]).astype(h_ref.dtype)


def _rmsnorm(x, g, tm):
    m, d = x.shape
    return pl.pallas_call(
        _rmsnorm_kernel,
        grid=(m // tm,),
        in_specs=[pl.BlockSpec((tm, d), lambda i: (i, 0)), pl.BlockSpec((1, d), lambda i: (0, 0))],
        out_specs=pl.BlockSpec((tm, d), lambda i: (i, 0)),
        out_shape=jax.ShapeDtypeStruct((m, d), BF16),
        compiler_params=_cparams("arbitrary"),
        name="rmsnorm",
    )(x, g)


def _conformer_kernel(*refs, rows, spt, tps, has_state, row_chunk):
    if has_state:
        (h_ref, wv_ref, wg_ref, wba_ref, wdw_ref, bdw_ref, gng_ref, gnb_ref, st_ref,
         a_ref, ba_ref, ns_ref, u_buf) = refs
    else:
        (h_ref, wv_ref, wg_ref, wba_ref, wdw_ref, bdw_ref, gng_ref, gnb_ref,
         a_ref, ba_ref, ns_ref, u_buf) = refs
        st_ref = None
    ch = wv_ref.shape[1]
    n_groups = ch // CONV_GROUP
    i = pl.program_id(0)

    h = h_ref[...]
    val = _dot(h, wv_ref[...])
    gate = _dot(h, wg_ref[...])
    ba_ref[...] = _dot(h, wba_ref[...])
    u = val * _sigmoid(gate)

    def init_history():
        for s in range(spt):
            u_buf[s, 0:HIST, :] = st_ref[s] if has_state else jnp.zeros((HIST, ch), F32)

    if tps > 1:
        pl.when(i % tps == 0)(init_history)
    else:
        init_history()
    for s in range(spt):
        u_buf[s, HIST:HIST + rows, :] = u[s * rows:(s + 1) * rows, :]

    n_rc = rows // row_chunk
    win = row_chunk + HIST
    lead = HIST - (CONV_TAPS - 1)

    def chunk_body(idx, carry):
        s = idx // n_rc
        t0 = pl.multiple_of((idx % n_rc) * row_chunk, row_chunk)
        for c in range(n_groups):
            lanes = slice(c * CONV_GROUP, (c + 1) * CONV_GROUP)
            x = u_buf[s, pl.ds(t0, win), lanes]
            acc = jnp.broadcast_to(bdw_ref[:, lanes], (row_chunk, CONV_GROUP))
            for shift in range(SUBLANES):
                xs = x if shift == 0 else pltpu.roll(x, win - shift, axis=0)
                for k in range(CONV_TAPS):
                    if (lead + k) % SUBLANES != shift:
                        continue
                    off = lead + k - shift
                    acc = acc + wdw_ref[k:k + 1, lanes] * xs[off:off + row_chunk, :]
            mu = jnp.mean(acc, axis=-1, keepdims=True)
            d = acc - mu
            var = jnp.mean(d * d, axis=-1, keepdims=True)
            un = d * lax.rsqrt(var + EPS) * gng_ref[:, lanes] + gnb_ref[:, lanes]
            r0 = pl.multiple_of(s * rows + t0, row_chunk)
            a_ref[pl.ds(r0, row_chunk), lanes] = (un * _sigmoid(un)).astype(a_ref.dtype)
        return carry

    lax.fori_loop(0, spt * n_rc, chunk_body, 0)

    for s in range(spt):
        tail = u_buf[s, rows:rows + HIST, :]
        ns_ref[s] = tail
        if tps > 1:
            u_buf[s, 0:HIST, :] = tail


def _conformer(h, wv, wg, wba, w_dw, b_dw, gn_g, gn_b, state, *, n_seq, seq_len, tm):
    m, d = h.shape
    ch = wv.shape[1]
    rows = min(tm, seq_len)
    spt = tm // rows
    tps = seq_len // rows
    assert m == n_seq * seq_len and m % tm == 0 and rows * spt == tm and rows * tps == seq_len
    assert spt == 1 or tps == 1
    row_chunk = min(64, rows)
    seq_block = (lambda i: (i // tps, 0, 0)) if tps > 1 else (lambda i: (i, 0, 0))
    const = lambda i: (0, 0)
    in_specs = [
        pl.BlockSpec((tm, d), lambda i: (i, 0)),
        pl.BlockSpec((d, ch), const), pl.BlockSpec((d, ch), const), pl.BlockSpec((d, LANES), const),
        pl.BlockSpec((CONV_TAPS, ch), const), pl.BlockSpec((1, ch), const),
        pl.BlockSpec((1, ch), const), pl.BlockSpec((1, ch), const),
    ]
    args = [h, wv, wg, wba, w_dw, b_dw, gn_g, gn_b]
    if state is not None:
        in_specs.append(pl.BlockSpec((spt, HIST, ch), seq_block))
        args.append(state)
    kern = functools.partial(_conformer_kernel, rows=rows, spt=spt, tps=tps,
                             has_state=state is not None, row_chunk=row_chunk)
    return pl.pallas_call(
        kern,
        grid=(m // tm,),
        in_specs=in_specs,
        out_specs=[pl.BlockSpec((tm, ch), lambda i: (i, 0)),
                   pl.BlockSpec((tm, LANES), lambda i: (i, 0)),
                   pl.BlockSpec((spt, HIST, ch), seq_block)],
        out_shape=[jax.ShapeDtypeStruct((m, ch), BF16),
                   jax.ShapeDtypeStruct((m, LANES), F32),
                   jax.ShapeDtypeStruct((n_seq, HIST, ch), F32)],
        scratch_shapes=[pltpu.VMEM((spt, HIST + rows, ch), F32)],
        compiler_params=_cparams("arbitrary"),
        name="conformer",
    )(*args)


def _matmul_kernel(h_ref, w_ref, o_ref):
    o_ref[...] = _dot(h_ref[...], w_ref[...]).astype(o_ref.dtype)


def _matmul(h, w, *, tm, tn):
    m, k = h.shape
    n = w.shape[1]
    assert m % tm == 0 and n % tn == 0
    return pl.pallas_call(
        _matmul_kernel,
        grid=(n // tn, m // tm),
        in_specs=[pl.BlockSpec((tm, k), lambda j, i: (i, 0)), pl.BlockSpec((k, tn), lambda j, i: (0, j))],
        out_specs=pl.BlockSpec((tm, tn), lambda j, i: (i, j)),
        out_shape=jax.ShapeDtypeStruct((m, n), F32),
        compiler_params=_cparams("arbitrary", "arbitrary"),
        name="qkvz_proj",
    )(h, w)


def _gdn_kernel(*refs, n_heads, n_chunks, sp, has_state):
    if has_state:
        (q_ref, k_ref, v_ref, z_ref, ba_ref, wsc_ref, alog_ref, dtb_ref, on_ref, st_ref, s0_ref,
         o_ref, nst_ref, sfin_ref, s_buf, hist) = refs
    else:
        (q_ref, k_ref, v_ref, z_ref, ba_ref, wsc_ref, alog_ref, dtb_ref, on_ref,
         o_ref, nst_ref, sfin_ref, s_buf, hist) = refs
        st_ref = s0_ref = None
    L = CHUNK
    gw = n_heads * HEAD_DIM
    n = pl.program_id(1)

    @pl.when(n == 0)
    def _():
        if has_state:
            s_buf[...] = s0_ref[...]
            for p in range(3):
                hist[:, p] = st_ref[:, :, p * gw:(p + 1) * gw]
        else:
            s_buf[...] = jnp.zeros(s_buf.shape, F32)
            hist[...] = jnp.zeros(hist.shape, F32)

    row = lax.broadcasted_iota(jnp.int32, (L, LANES), 0)
    lane = lax.broadcasted_iota(jnp.int32, (L, LANES), 1)
    incl = (lane <= row)
    strict = (lane < row)
    eye_hi = (lane == row + L).astype(F32)
    right = lane >= L
    zeros_k = jnp.zeros((L, HEAD_DIM), BF16)

    def sequence_prologue(s):
        conv = []
        for p, x_ref in enumerate((q_ref, k_ref, v_ref)):
            x = x_ref[s]
            xe = jnp.concatenate([hist[s, p], x], axis=0)
            w = wsc_ref[:, p * gw:(p + 1) * gw]
            y = w[SHORT_TAPS - 1:SHORT_TAPS, :] * x
            for j in range(1, SHORT_TAPS):
                xj = pltpu.roll(xe, j, axis=0)[SUBLANES:SUBLANES + L, :]
                y = y + w[SHORT_TAPS - 1 - j:SHORT_TAPS - j, :] * xj
            conv.append(y * _sigmoid(y))
            tail = x[L - SUBLANES:L, :]
            hist[s, p] = tail
            nst_ref[s, :, p * gw:(p + 1) * gw] = tail
        ba = ba_ref[s]
        beta_all = _sigmoid(ba)
        gcum = -jnp.exp(alog_ref[...]) * _softplus(ba + dtb_ref[...])
        d = 1
        while d < L:
            gcum = gcum + jnp.where(row >= d, pltpu.roll(gcum, d, axis=0), 0.0)
            d *= 2
        g_last = gcum[L - 1:L, :]
        gates = dict(beta=beta_all, gcum=gcum, eg=jnp.exp(gcum), ekd=jnp.exp(g_last - gcum), gl=jnp.exp(g_last),
                     gcum_t=jnp.concatenate([gcum, jnp.zeros((LANES - L, LANES), F32)], axis=0).T)
        return conv, gates

    def head_chain(s, hd, conv, gates):
        qc, kc, vc = conv
        lanes = slice(hd * HEAD_DIM, (hd + 1) * HEAD_DIM)
        cb, ca = hd, n_heads + hd
        beta = gates["beta"][:, cb:cb + 1]
        gcol = gates["gcum"][:, ca:ca + 1]
        grow = gates["gcum_t"][ca:ca + 1, :]
        eg = gates["eg"][:, ca:ca + 1]
        ekd = gates["ekd"][:, ca:ca + 1]
        gl = gates["gl"][:, ca:ca + 1]

        q = qc[:, lanes]
        k = kc[:, lanes]
        v = vc[:, lanes]
        qn = q * lax.rsqrt(jnp.sum(q * q, axis=-1, keepdims=True) + EPS) * (HEAD_DIM ** -0.5)
        kn = k * lax.rsqrt(jnp.sum(k * k, axis=-1, keepdims=True) + EPS)
        kb = kn * beta
        a_raw = _dot_nt(jnp.concatenate([kb, qn], axis=0).astype(BF16),
                        jnp.concatenate([kn.astype(BF16), zeros_k], axis=0))
        yield
        decay = jnp.exp(jnp.where(incl, gcol - grow, -jnp.inf))
        qk = a_raw[L:2 * L, :] * decay
        x = jnp.where(strict, -(a_raw[0:L, :] * decay), 0.0) + eye_hi
        for _ in range(5):
            x = _dot(x[:, 0:L].astype(BF16), x.astype(BF16)) + jnp.where(right, x, 0.0)
            yield
        xr = pltpu.roll(x, L, axis=1)
        t = _dot(x[:, 0:L].astype(BF16), xr.astype(BF16)) + xr
        yield
        rhs = jnp.concatenate([v * beta, kb * eg], axis=1).astype(BF16)
        uw = _dot(t[:, 0:L].astype(BF16), rhs)
        yield
        u = uw[:, 0:HEAD_DIM]
        w = uw[:, HEAD_DIM:2 * HEAD_DIM]
        st = s_buf[s, hd]
        m1 = _dot(jnp.concatenate([w, qn * eg], axis=0).astype(BF16), st.astype(BF16))
        yield
        vn_bf = (u - m1[0:L, :]).astype(BF16)
        o = m1[L:2 * L, :] + _dot(qk[:, 0:L].astype(BF16), vn_bf)
        yield
        s_buf[s, hd] = st * gl + _dot_tn((kn * ekd).astype(BF16), vn_bf)
        o = o * lax.rsqrt(jnp.mean(o * o, axis=-1, keepdims=True) + EPS) * on_ref[...]
        zz = z_ref[s, :, lanes]
        o_ref[s, :, lanes] = (o * (zz * _sigmoid(zz))).astype(o_ref.dtype)

    chains = []
    for s in range(sp):
        conv, gates = sequence_prologue(s)
        chains += [head_chain(s, hd, conv, gates) for hd in range(n_heads)]
    while chains:
        alive = []
        for c in chains:
            try:
                next(c)
                alive.append(c)
            except StopIteration:
                pass
        chains = alive

    @pl.when(n == n_chunks - 1)
    def _():
        sfin_ref[...] = s_buf[...]


def _gdn(qkvz, ba, w_sc, alog_row, dtb_row, onorm_g, st_qkv, s0, *, n_seq, seq_len, n_heads, sp):
    m = qkvz.shape[0]
    gw = n_heads * HEAD_DIM
    n_chunks = seq_len // CHUNK
    assert m == n_seq * seq_len and n_chunks * CHUNK == seq_len and n_seq % sp == 0
    qkvz3 = qkvz.reshape(n_seq, seq_len, 4 * gw)
    ba3 = ba.reshape(n_seq, seq_len, LANES)
    const = lambda b, n: (0, 0)
    in_specs = [pl.BlockSpec((sp, CHUNK, gw), lambda b, n, p=p: (b, n, p)) for p in range(4)]
    in_specs += [
        pl.BlockSpec((sp, CHUNK, LANES), lambda b, n: (b, n, 0)),
        pl.BlockSpec((SHORT_TAPS, 3 * gw), const),
        pl.BlockSpec((1, LANES), const), pl.BlockSpec((1, LANES), const), pl.BlockSpec((1, HEAD_DIM), const),
    ]
    args = [qkvz3, qkvz3, qkvz3, qkvz3, ba3, w_sc, alog_row, dtb_row, onorm_g]
    has_state = st_qkv is not None
    if has_state:
        in_specs += [pl.BlockSpec((sp, SUBLANES, 3 * gw), lambda b, n: (b, 0, 0)),
                     pl.BlockSpec((sp, n_heads, HEAD_DIM, HEAD_DIM), lambda b, n: (b, 0, 0, 0))]
        args += [st_qkv, s0]
    kern = functools.partial(_gdn_kernel, n_heads=n_heads, n_chunks=n_chunks, sp=sp, has_state=has_state)
    b_out, ns_qkv, s_fin = pl.pallas_call(
        kern,
        grid=(n_seq // sp, n_chunks),
        in_specs=in_specs,
        out_specs=[pl.BlockSpec((sp, CHUNK, gw), lambda b, n: (b, n, 0)),
                   pl.BlockSpec((sp, SUBLANES, 3 * gw), lambda b, n: (b, 0, 0)),
                   pl.BlockSpec((sp, n_heads, HEAD_DIM, HEAD_DIM), lambda b, n: (b, 0, 0, 0))],
        out_shape=[jax.ShapeDtypeStruct((n_seq, seq_len, gw), BF16),
                   jax.ShapeDtypeStruct((n_seq, SUBLANES, 3 * gw), F32),
                   jax.ShapeDtypeStruct((n_seq, n_heads, HEAD_DIM, HEAD_DIM), F32)],
        scratch_shapes=[pltpu.VMEM((sp, n_heads, HEAD_DIM, HEAD_DIM), F32),
                        pltpu.VMEM((sp, 3, SUBLANES, gw), F32)],
        compiler_params=_cparams("arbitrary", "arbitrary"),
        name="gdn",
    )(*args)
    return b_out.reshape(m, gw), ns_qkv, s_fin


def _outproj_kernel(a_ref, b_ref, wa_ref, wb_ref, x_ref, gpost_ref, gnext_ref, xo_ref, ho_ref):
    y = _dot(a_ref[...], wa_ref[...]) + _dot(b_ref[...], wb_ref[...])
    xn = x_ref[...] + _rms(y, gpost_ref[...])
    xo_ref[...] = xn
    ho_ref[...] = _rms(xn, gnext_ref[...]).astype(ho_ref.dtype)


def _outproj(a, b, w_out, x, g_post, g_next, *, tm):
    m, ka = a.shape
    kb = b.shape[1]
    d = w_out.shape[1]
    assert w_out.shape[0] == ka + kb and ka == kb and m % tm == 0
    row = lambda i: (i, 0)
    const = lambda i: (0, 0)
    return pl.pallas_call(
        _outproj_kernel,
        grid=(m // tm,),
        in_specs=[pl.BlockSpec((tm, ka), row), pl.BlockSpec((tm, kb), row),
                  pl.BlockSpec((ka, d), lambda i: (0, 0)), pl.BlockSpec((kb, d), lambda i: (1, 0)),
                  pl.BlockSpec((tm, d), row), pl.BlockSpec((1, d), const), pl.BlockSpec((1, d), const)],
        out_specs=[pl.BlockSpec((tm, d), row), pl.BlockSpec((tm, d), row)],
        out_shape=[jax.ShapeDtypeStruct((m, d), F32), jax.ShapeDtypeStruct((m, d), BF16)],
        compiler_params=_cparams("arbitrary"),
        name="outproj",
    )(a, b, w_out, w_out, x, g_post, g_next)


def _ffn_up_kernel(*refs, rows, spt, tps, has_state):
    if has_state:
        (h_ref, wg_ref, wu_ref, cwg_ref, cwu_ref, cbg_ref, cbu_ref, stg_ref, stu_ref,
         act_ref, nsg_ref, nsu_ref, carry) = refs
    else:
        (h_ref, wg_ref, wu_ref, cwg_ref, cwu_ref, cbg_ref, cbu_ref,
         act_ref, nsg_ref, nsu_ref, carry) = refs
        stg_ref = stu_ref = None
    tm = rows * spt
    tn = wg_ref.shape[1]
    i = pl.program_id(1)
    h = h_ref[...]
    assert rows & (rows - 1) == 0
    pos = lax.broadcasted_iota(jnp.int32, (tm, tn), 0) & (rows - 1)

    if tps > 1:
        @pl.when(i % tps == 0)
        def _():
            carry[...] = jnp.zeros(carry.shape, F32)

    outs = []
    for part, (w_ref, cw_ref, cb_ref, st_ref, ns_ref) in enumerate((
            (wg_ref, cwg_ref, cbg_ref, stg_ref, nsg_ref), (wu_ref, cwu_ref, cbu_ref, stu_ref, nsu_ref))):
        g = _dot(h, w_ref[...])
        prev1, prev2 = [], []
        for s in range(spt):
            if has_state:
                hrows = st_ref[s]
            elif tps > 1:
                hrows = carry[part]
            else:
                hrows = jnp.zeros((SUBLANES, tn), F32)
            prev1.append(jnp.broadcast_to(hrows[SUBLANES - 1:SUBLANES, :], (rows, tn)))
            prev2.append(jnp.broadcast_to(hrows[SUBLANES - 2:SUBLANES - 1, :], (rows, tn)))
        prev1 = prev1[0] if spt == 1 else jnp.concatenate(prev1, axis=0)
        prev2 = prev2[0] if spt == 1 else jnp.concatenate(prev2, axis=0)
        g1 = jnp.where(pos == 0, prev1, pltpu.roll(g, 1, axis=0))
        g2 = jnp.where(pos == 0, prev2, jnp.where(pos == 1, prev1, pltpu.roll(g, 2, axis=0)))
        cw = cw_ref[...]
        outs.append(cb_ref[...] + cw[2:3, :] * g + cw[1:2, :] * g1 + cw[0:1, :] * g2)
        for s in range(spt):
            tail = g[(s + 1) * rows - SUBLANES:(s + 1) * rows, :]
            ns_ref[s] = tail
        if tps > 1:
            carry[part] = g[tm - SUBLANES:tm, :]
    yg, yu = outs
    act_ref[...] = (yg * _sigmoid(yg) * yu).astype(act_ref.dtype)


def _ffn_up(h, w_up, cw, cb, state, *, n_seq, seq_len, tm, tn):
    m, d = h.shape
    ffn = w_up.shape[1] // 2
    nt = ffn // tn
    rows = min(tm, seq_len)
    spt = tm // rows
    tps = seq_len // rows
    assert m == n_seq * seq_len and m % tm == 0 and nt * tn == ffn and (spt == 1 or tps == 1)
    has_state = state is not None
    seq_idx = (lambda i: i // tps) if tps > 1 else (lambda i: i)
    in_specs = [
        pl.BlockSpec((tm, d), lambda j, i: (i, 0)),
        pl.BlockSpec((d, tn), lambda j, i: (0, j)), pl.BlockSpec((d, tn), lambda j, i: (0, j + nt)),
        pl.BlockSpec((FFN_TAPS, tn), lambda j, i: (0, j)), pl.BlockSpec((FFN_TAPS, tn), lambda j, i: (0, j + nt)),
        pl.BlockSpec((1, tn), lambda j, i: (0, j)), pl.BlockSpec((1, tn), lambda j, i: (0, j + nt)),
    ]
    args = [h, w_up, w_up, cw, cw, cb, cb]
    if has_state:
        in_specs += [pl.BlockSpec((spt, SUBLANES, tn), lambda j, i: (seq_idx(i), 0, j)),
                     pl.BlockSpec((spt, SUBLANES, tn), lambda j, i: (seq_idx(i), 0, j + nt))]
        args += [state, state]
    kern = functools.partial(_ffn_up_kernel, rows=rows, spt=spt, tps=tps, has_state=has_state)
    ns_spec = pl.BlockSpec((spt, SUBLANES, tn), lambda j, i: (seq_idx(i), 0, j))
    return pl.pallas_call(
        kern,
        grid=(nt, m // tm),
        in_specs=in_specs,
        out_specs=[pl.BlockSpec((tm, tn), lambda j, i: (i, j)), ns_spec, ns_spec],
        out_shape=[jax.ShapeDtypeStruct((m, ffn), BF16),
                   jax.ShapeDtypeStruct((n_seq, SUBLANES, ffn), F32),
                   jax.ShapeDtypeStruct((n_seq, SUBLANES, ffn), F32)],
        scratch_shapes=[pltpu.VMEM((2, SUBLANES, tn), F32)],
        compiler_params=_cparams("arbitrary", "arbitrary"),
        name="ffn_up",
    )(*args)


def _ffn_down_kernel(act_ref, w_ref, x_ref, gpost_ref, gnext_ref, xo_ref, ho_ref, acc):
    kk = pl.program_id(1)

    @pl.when(kk == 0)
    def _():
        acc[...] = jnp.zeros(acc.shape, F32)

    acc[...] += _dot(act_ref[...], w_ref[...])

    @pl.when(kk == pl.num_programs(1) - 1)
    def _():
        xn = x_ref[...] + _rms(acc[...], gpost_ref[...])
        xo_ref[...] = xn
        ho_ref[...] = _rms(xn, gnext_ref[...]).astype(ho_ref.dtype)


def _ffn_down(act, w_down, x, g_post, g_next, *, tm, tk):
    m, k = act.shape
    d = w_down.shape[1]
    assert m % tm == 0 and k % tk == 0
    row = lambda i, kk: (i, 0)
    const = lambda i, kk: (0, 0)
    return pl.pallas_call(
        _ffn_down_kernel,
        grid=(m // tm, k // tk),
        in_specs=[pl.BlockSpec((tm, tk), lambda i, kk: (i, kk)), pl.BlockSpec((tk, d), lambda i, kk: (kk, 0)),
                  pl.BlockSpec((tm, d), row), pl.BlockSpec((1, d), const), pl.BlockSpec((1, d), const)],
        out_specs=[pl.BlockSpec((tm, d), row), pl.BlockSpec((tm, d), row)],
        out_shape=[jax.ShapeDtypeStruct((m, d), F32), jax.ShapeDtypeStruct((m, d), BF16)],
        scratch_shapes=[pltpu.VMEM((tm, d), F32)],
        compiler_params=_cparams("arbitrary", "arbitrary"),
        name="ffn_down",
    )(act, w_down, x, g_post, g_next)


def _pad_front(state, rows):
    return jnp.pad(state, ((0, 0), (rows - state.shape[1], 0), (0, 0)))


def _layer(x, h, states, p, g_next, *, n_seq, seq_len, tm, gdn_sp=2):
    n_heads = p["a_log"].shape[-1]
    gw = n_heads * HEAD_DIM
    if states is None:
        st_conv = st_qkv = st_gdn = st_ffn = None
    else:
        st_conv = _pad_front(states[0], HIST)
        st_qkv = _pad_front(states[1], SUBLANES)
        st_gdn = states[2]
        st_ffn = _pad_front(states[3], SUBLANES)

    a_out, ba, ns_conv = _conformer(h, p["wv"], p["wg"], p["wba"], p["w_dw"], p["b_dw"], p["gn_g"], p["gn_b"],
                                    st_conv, n_seq=n_seq, seq_len=seq_len, tm=tm)
    qkvz = _matmul(h, p["wqkvz"], tm=2 * tm, tn=1024)
    b_out, ns_qkv, ns_gdn = _gdn(qkvz, ba, p["w_sc"], p["alog_row"], p["dtb_row"], p["onorm_g"], st_qkv, st_gdn,
                                 n_seq=n_seq, seq_len=seq_len, n_heads=n_heads, sp=gdn_sp)
    x, h = _outproj(a_out, b_out, p["w_out"], x, p["g_post_mix"], p["g_pre_ffn"], tm=tm)
    act, ns_fg, ns_fu = _ffn_up(h, p["w_up"], p["w_ffn_dw"], p["b_ffn_dw"], st_ffn,
                                n_seq=n_seq, seq_len=seq_len, tm=2 * tm, tn=512)
    x, h = _ffn_down(act, p["w_down"], x, p["g_post_ffn"], g_next, tm=tm, tk=1408)
    new_states = (ns_conv[:, HIST - (CONV_TAPS - 1):, :],
                  ns_qkv[:, SUBLANES - (SHORT_TAPS - 1):, :],
                  ns_gdn,
                  jnp.concatenate([ns_fg, ns_fu], axis=-1)[:, SUBLANES - (FFN_TAPS - 1):, :])
    return x, h, new_states


def _layer_params(l, g_pre_mix, w_in, w_dw, b_dw, gn_g, gn_b, w_sc, a_log, dt_bias, onorm_g,
                  w_out, g_post_mix, g_pre_ffn, w_up, w_ffn_dw, b_ffn_dw, w_down, g_post_ffn):
    ch = w_dw.shape[-1]
    n_heads = a_log.shape[-1]
    gw = n_heads * HEAD_DIM
    c1 = 2 * ch
    c3 = c1 + 4 * gw
    wl = w_in[l]
    wba = jnp.pad(wl[:, c3:c3 + 2 * n_heads], ((0, 0), (0, LANES - 2 * n_heads)))
    row = lambda v: v[l][None, :]
    gate_cols = lambda v: jnp.pad(v[l], (n_heads, LANES - 2 * n_heads))[None, :]
    return dict(
        wv=wl[:, :ch].astype(BF16), wg=wl[:, ch:c1].astype(BF16), wqkvz=wl[:, c1:c3].astype(BF16),
        wba=wba.astype(BF16),
        w_dw=w_dw[l], b_dw=row(b_dw), gn_g=row(gn_g), gn_b=row(gn_b),
        w_sc=w_sc[l], alog_row=gate_cols(a_log), dtb_row=gate_cols(dt_bias), onorm_g=row(onorm_g),
        a_log=a_log[l],
        w_out=w_out[l].astype(BF16), g_post_mix=row(g_post_mix), g_pre_ffn=row(g_pre_ffn),
        w_up=w_up[l].astype(BF16), w_ffn_dw=w_ffn_dw[l], b_ffn_dw=row(b_ffn_dw),
        w_down=w_down[l].astype(BF16), g_post_ffn=row(g_post_ffn),
    )


def kernel(x_prompt, x_sample, state_conv, state_qkv_conv, state_gdn, state_ffn_conv, g_pre_mix, w_in, w_dw, b_dw, gn_g, gn_b, w_sc, a_log, dt_bias, onorm_g, w_out, g_post_mix, g_pre_ffn, w_up, w_ffn_dw, b_ffn_dw, w_down, g_post_ffn):
    depth = w_in.shape[0]
    bp, tp, d = x_prompt.shape
    bs, ts, _ = x_sample.shape
    tm = 512
    xp = x_prompt.reshape(bp * tp, d)
    xs = x_sample.reshape(bs * ts, d)
    hp = _rmsnorm(xp, g_pre_mix[0][None, :], tm)
    hs = _rmsnorm(xs, g_pre_mix[0][None, :], tm)
    outs_p, outs_s = [], []
    for l in range(depth):
        p = _layer_params(l, g_pre_mix, w_in, w_dw, b_dw, gn_g, gn_b, w_sc, a_log, dt_bias, onorm_g,
                          w_out, g_post_mix, g_pre_ffn, w_up, w_ffn_dw, b_ffn_dw, w_down, g_post_ffn)
        g_next = g_pre_mix[(l + 1) % depth][None, :]
        xp, hp, st_p = _layer(xp, hp, None, p, g_next, n_seq=bp, seq_len=tp, tm=tm)
        xs, hs, st_s = _layer(xs, hs, (state_conv[l], state_qkv_conv[l], state_gdn[l], state_ffn_conv[l]),
                              p, g_next, n_seq=bs, seq_len=ts, tm=tm)
        outs_p.append(st_p)
        outs_s.append(st_s)
    stack = lambda outs, j: jnp.stack([o[j] for o in outs])
    return (xp.reshape(bp, tp, d), xs.reshape(bs, ts, d),
            stack(outs_p, 0), stack(outs_p, 1), stack(outs_p, 2), stack(outs_p, 3),
            stack(outs_s, 0), stack(outs_s, 1), stack(outs_s, 2), stack(outs_s, 3))
```
